```python
import jax
import jax.numpy as jnp
from jax import lax
import numpy as np

D_MODEL = 2048
BATCH = 1
SEQ = 8192
DEPTH = 1
DEC_BATCH = 1
DEC_SEQ = 16384
PAST_LEN = 128

CONV_DIM = D_MODEL
CONV_K = 31
SSM_DIM = 2 * D_MODEL
HEAD_DIM = 64
N_HEADS = SSM_DIM // HEAD_DIM
N_GROUPS = 8
HEADS_PER_GROUP = N_HEADS // N_GROUPS
D_STATE = 128
SSM_CONV_K = 5
XBC_DIM = SSM_DIM + 2 * N_GROUPS * D_STATE
NORM_GROUP = SSM_DIM // N_GROUPS
CHUNK = 128
EPS = 1e-5
OFF_CV = 0
OFF_CG = OFF_CV + CONV_DIM
OFF_CS = OFF_CG + CONV_DIM
OFF_Z = OFF_CS + CONV_DIM
OFF_XBC = OFF_Z + SSM_DIM
OFF_DT = OFF_XBC + XBC_DIM
OFF_GATE = OFF_DT + 2 * N_HEADS
IN_DIM = OFF_GATE + 2 * D_MODEL

kernel_name = "bidir_conformer_conv_mamba2_gated_hybrid"


def rms_norm(x, w):
    xf = x.astype(jnp.float32)
    y = xf * lax.rsqrt(jnp.mean(xf * xf, axis=-1, keepdims=True) + EPS)
    return (y * w.astype(jnp.float32)).astype(x.dtype)


def layer_norm(x, g, b):
    xf = x.astype(jnp.float32)
    mu = jnp.mean(xf, axis=-1, keepdims=True)
    xc = xf - mu
    y = xc * lax.rsqrt(jnp.mean(xc * xc, axis=-1, keepdims=True) + EPS)
    return (y * g.astype(jnp.float32) + b.astype(jnp.float32)).astype(x.dtype)


def depthwise_conv(x, w, b):
    k = w.shape[0]
    y = lax.conv_general_dilated(
        x, w[:, None, :].astype(x.dtype), window_strides=(1,),
        padding=[(k // 2, k // 2)], dimension_numbers=('NWC', 'WIO', 'NWC'),
        feature_group_count=x.shape[-1])
    return y + b.astype(x.dtype)


def ssd_scan(x, dt, a, bm, cm):
    b, s = x.shape[0], x.shape[1]
    nc = s // CHUNK
    xr = x.reshape(b, nc, CHUNK, N_GROUPS, HEADS_PER_GROUP, HEAD_DIM).transpose(1, 0, 2, 3, 4, 5)
    dtr = dt.reshape(b, nc, CHUNK, N_GROUPS, HEADS_PER_GROUP).transpose(1, 0, 2, 3, 4)
    br = bm.reshape(b, nc, CHUNK, N_GROUPS, D_STATE).transpose(1, 0, 2, 3, 4)
    cr = cm.reshape(b, nc, CHUNK, N_GROUPS, D_STATE).transpose(1, 0, 2, 3, 4)
    ag = a.reshape(N_GROUPS, HEADS_PER_GROUP)
    mask = jnp.tril(jnp.ones((CHUNK, CHUNK), dtype=bool))[None, :, :, None, None]

    def step(state, inp):
        xc, dtc, bc, cc = inp
        cs = jnp.cumsum(dtc * ag, axis=1)
        seg = cs[:, :, None] - cs[:, None]
        decay = jnp.exp(jnp.where(mask, seg, -jnp.inf))
        cb = jnp.einsum('blgn,bsgn->blsg', cc, bc)
        y = jnp.einsum('blsg,blsgh,bsgh,bsghp->blghp', cb, decay, dtc, xc)
        y = y + jnp.einsum('blgn,bghpn,blgh->blghp', cc, state, jnp.exp(cs))
        last = cs[:, -1]
        w = jnp.exp(last[:, None] - cs) * dtc
        state = state * jnp.exp(last)[..., None, None] + jnp.einsum('bsgn,bsgh,bsghp->bghpn', bc, w, xc)
        return state, y

    state0 = jnp.zeros((b, N_GROUPS, HEADS_PER_GROUP, HEAD_DIM, D_STATE), jnp.float32)
    _, ys = lax.scan(step, state0, (xr, dtr, br, cr))
    return ys.transpose(1, 0, 2, 3, 4, 5).reshape(b, s, N_HEADS, HEAD_DIM)


def encoder_layer(x, norm_w, w_in, b_gate, dw_w, dw_b, ln_g, ln_b, sconv_w, sconv_b,
                  dt_bias, a_log, d_skip, ssm_norm_w, w_branch, w_out):
    bsz, s = x.shape[0], x.shape[1]
    f32 = jnp.float32
    h = rms_norm(x, norm_w)
    proj = jnp.einsum('bsd,de->bse', h, w_in)
    cv = proj[..., OFF_CV:OFF_CG]
    cg = proj[..., OFF_CG:OFF_CS]
    cgate = proj[..., OFF_CS:OFF_Z]
    z = proj[..., OFF_Z:OFF_XBC]
    xbc = proj[..., OFF_XBC:OFF_DT]
    dt_raw = proj[..., OFF_DT:OFF_GATE]
    gate_raw = proj[..., OFF_GATE:]

    u = cv * jax.nn.sigmoid(cg)
    u = depthwise_conv(u, dw_w, dw_b)
    u = jax.nn.silu(layer_norm(u, ln_g, ln_b))
    y_c = u * jax.nn.silu(cgate)

    xbc = jax.nn.silu(depthwise_conv(xbc, sconv_w, sconv_b))
    xs = xbc[..., :SSM_DIM].astype(f32).reshape(bsz, s, N_HEADS, HEAD_DIM)
    bm = xbc[..., SSM_DIM:SSM_DIM + N_GROUPS * D_STATE].astype(f32).reshape(bsz, s, N_GROUPS, D_STATE)
    cm = xbc[..., SSM_DIM + N_GROUPS * D_STATE:].astype(f32).reshape(bsz, s, N_GROUPS, D_STATE)
    dt = jax.nn.softplus(dt_raw.astype(f32).reshape(bsz, s, 2, N_HEADS) + dt_bias.astype(f32))
    a = -jnp.exp(a_log.astype(f32))
    flip = lambda t: jnp.flip(t, axis=1)
    y_f = ssd_scan(xs, dt[:, :, 0], a[0], bm, cm)
    y_b = flip(ssd_scan(flip(xs), flip(dt[:, :, 1]), a[1], flip(bm), flip(cm)))
    y = y_f + y_b + xs * d_skip.astype(f32)[:, None]
    y = y.reshape(bsz, s, SSM_DIM) * jax.nn.silu(z.astype(f32))
    yg = y.reshape(bsz, s, N_GROUPS, NORM_GROUP)
    yg = yg * lax.rsqrt(jnp.mean(yg * yg, axis=-1, keepdims=True) + EPS)
    y_s = (yg.reshape(bsz, s, SSM_DIM) * ssm_norm_w.astype(f32)).astype(x.dtype)

    o_c = jnp.einsum('bse,ed->bsd', y_c, w_branch[:CONV_DIM])
    o_s = jnp.einsum('bse,ed->bsd', y_s, w_branch[CONV_DIM:])
    g = jax.nn.sigmoid(gate_raw + b_gate).reshape(bsz, s, 2, D_MODEL)
    m = g[..., 0, :] * o_c + g[..., 1, :] * o_s
    return x + jnp.einsum('bsd,de->bse', m, w_out)


def trunk(x, norm_w, w_in, b_gate, dw_w, dw_b, ln_g, ln_b, sconv_w, sconv_b,
          dt_bias, a_log, d_skip, ssm_norm_w, w_branch, w_out, final_norm_w):
    for l in range(DEPTH):
        x = encoder_layer(x, norm_w[l], w_in[l], b_gate[l], dw_w[l], dw_b[l], ln_g[l], ln_b[l],
                          sconv_w[l], sconv_b[l], dt_bias[l], a_log[l], d_skip[l],
                          ssm_norm_w[l], w_branch[l], w_out[l])
    return rms_norm(x, final_norm_w)


def setup_inputs(seed: int = 0) -> dict:
    key = jax.random.key(seed)
    k = jax.random.split(key, 20)
    f32 = jnp.float32
    nrm = lambda kk, shape, scale: jax.random.normal(kk, shape, f32) * scale
    dt0 = jnp.exp(jax.random.uniform(k[9], (DEPTH, 2, N_HEADS), f32) * (np.log(0.1) - np.log(0.001)) + np.log(0.001))
    dt_bias = dt0 + jnp.log(-jnp.expm1(-dt0))
    return {
        "x_prompt": jax.random.normal(k[0], (BATCH, SEQ, D_MODEL), f32),
        "x_sample": jax.random.normal(k[1], (DEC_BATCH, DEC_SEQ, D_MODEL), f32),
        "norm_w": 1.0 + nrm(k[2], (DEPTH, D_MODEL), 0.02),
        "w_in": nrm(k[3], (DEPTH, D_MODEL, IN_DIM), D_MODEL ** -0.5),
        "b_gate": nrm(k[4], (DEPTH, 2 * D_MODEL), 0.1),
        "dw_w": nrm(k[5], (DEPTH, CONV_K, CONV_DIM), CONV_K ** -0.5),
        "dw_b": nrm(k[6], (DEPTH, CONV_DIM), 0.02),
        "ln_g": 1.0 + nrm(k[7], (DEPTH, CONV_DIM), 0.02),
        "ln_b": nrm(k[8], (DEPTH, CONV_DIM), 0.02),
        "sconv_w": nrm(k[10], (DEPTH, SSM_CONV_K, XBC_DIM), SSM_CONV_K ** -0.5),
        "sconv_b": nrm(k[11], (DEPTH, XBC_DIM), 0.02),
        "dt_bias": dt_bias,
        "a_log": jnp.log(jax.random.uniform(k[12], (DEPTH, 2, N_HEADS), f32, 1.0, 16.0)),
        "d_skip": 1.0 + nrm(k[13], (DEPTH, N_HEADS), 0.1),
        "ssm_norm_w": 1.0 + nrm(k[14], (DEPTH, SSM_DIM), 0.02),
        "w_branch": jnp.concatenate([nrm(k[15], (DEPTH, CONV_DIM, D_MODEL), CONV_DIM ** -0.5),
                                     nrm(k[16], (DEPTH, SSM_DIM, D_MODEL), SSM_DIM ** -0.5)], axis=1),
        "w_out": nrm(k[17], (DEPTH, D_MODEL, D_MODEL), D_MODEL ** -0.5),
        "final_norm_w": 1.0 + nrm(k[18], (D_MODEL,), 0.02),
    }


def reference(x_prompt, x_sample, norm_w, w_in, b_gate, dw_w, dw_b, ln_g, ln_b, sconv_w, sconv_b,
              dt_bias, a_log, d_skip, ssm_norm_w, w_branch, w_out, final_norm_w):
    y_prompt = trunk(x_prompt, norm_w, w_in, b_gate, dw_w, dw_b, ln_g, ln_b, sconv_w, sconv_b,
                     dt_bias, a_log, d_skip, ssm_norm_w, w_branch, w_out, final_norm_w)
    y_sample = trunk(x_sample, norm_w, w_in, b_gate, dw_w, dw_b, ln_g, ln_b, sconv_w, sconv_b,
                     dt_bias, a_log, d_skip, ssm_norm_w, w_branch, w_out, final_norm_w)
    return (y_prompt, y_sample)
```

```python
import functools

import jax
import jax.numpy as jnp
import numpy as np
from jax import lax
from jax.experimental import pallas as pl
from jax.experimental.pallas import tpu as pltpu

F32 = jnp.float32
BF16 = jnp.bfloat16

D_MODEL = 2048
CONV_DIM = D_MODEL
CONV_K = 31
SSM_DIM = 2 * D_MODEL
HEAD_DIM = 64
N_HEADS = SSM_DIM // HEAD_DIM
N_GROUPS = 8
HEADS_PER_GROUP = N_HEADS // N_GROUPS
D_STATE = 128
SSM_CONV_K = 5
BC_DIM = N_GROUPS * D_STATE
XBC_DIM = SSM_DIM + 2 * BC_DIM
NORM_GROUP = SSM_DIM // N_GROUPS
GROUP_W = HEADS_PER_GROUP * HEAD_DIM
EPS = 1e-5
OFF_CV = 0
OFF_CG = OFF_CV + CONV_DIM
OFF_CS = OFF_CG + CONV_DIM
OFF_Z = OFF_CS + CONV_DIM
OFF_XBC = OFF_Z + SSM_DIM
OFF_DT = OFF_XBC + XBC_DIM
OFF_GATE = OFF_DT + 2 * N_HEADS
IN_DIM = OFF_GATE + 2 * D_MODEL

LANES = 128
HALO = 16
SCAN_L = 128
VMEM_LIMIT = 56 * 1024 * 1024


def _cparams(*sem):
    return pltpu.CompilerParams(dimension_semantics=sem, vmem_limit_bytes=VMEM_LIMIT)


def _sigmoid(v):
    return 1.0 / (1.0 + jnp.exp(-v))


def _silu(v):
    return v * _sigmoid(v)


def _softplus(v):
    return jnp.maximum(v, 0.0) + jnp.log(1.0 + jnp.exp(-jnp.abs(v)))


def _split3(v):
    a1 = v.astype(BF16)
    r1 = v - a1.astype(F32)
    a2 = r1.astype(BF16)
    a3 = (r1 - a2.astype(F32)).astype(BF16)
    return a1, a2, a3


def _dot(a, b):
    return jnp.dot(a, b, preferred_element_type=F32)


def _rms_kernel(x_ref, w_ref, o_ref):
    x = x_ref[...]
    ms = jnp.mean(x * x, axis=-1, keepdims=True)
    o_ref[...] = (x * lax.rsqrt(ms + EPS) * w_ref[...]).astype(o_ref.dtype)


def _rms_call(x, w, tm=512):
    s, d = x.shape
    return pl.pallas_call(
        _rms_kernel,
        grid=(s // tm,),
        in_specs=[pl.BlockSpec((tm, d), lambda i: (i, 0)), pl.BlockSpec((1, d), lambda i: (0, 0))],
        out_specs=pl.BlockSpec((tm, d), lambda i: (i, 0)),
        out_shape=jax.ShapeDtypeStruct((s, d), BF16),
        compiler_params=_cparams("parallel"),
        name="rms_in",
    )(x, w.reshape(1, d))


def _proj_kernel(h_ref, *refs, n_w, n_extra, epilogue):
    w_refs = refs[:n_w]
    extra = refs[n_w:n_w + n_extra]
    out_refs = refs[n_w + n_extra:]
    h = h_ref[...]
    accs = [_dot(h, w[...]) for w in w_refs]
    outs = epilogue(*accs, *[e[...] for e in extra])
    for o_ref, o in zip(out_refs, outs):
        o_ref[...] = o.astype(o_ref.dtype)


def _proj_call(h, ws, extras, epilogue, n_out, name, tm=1024, tn=512):
    s, d = h.shape
    n = ws[0].shape[1]
    kern = functools.partial(_proj_kernel, n_w=len(ws), n_extra=len(extras), epilogue=epilogue)
    return pl.pallas_call(
        kern,
        grid=(s // tm, n // tn),
        in_specs=([pl.BlockSpec((tm, d), lambda i, j: (i, 0))]
                  + [pl.BlockSpec((d, tn), lambda i, j: (0, j)) for _ in ws]
                  + [pl.BlockSpec((1, tn), lambda i, j: (0, j)) for _ in extras]),
        out_specs=[pl.BlockSpec((tm, tn), lambda i, j: (i, j)) for _ in range(n_out)],
        out_shape=[jax.ShapeDtypeStruct((s, n), BF16) for _ in range(n_out)],
        compiler_params=_cparams("parallel", "arbitrary"),
        name=name,
    )(h, *ws, *extras)


def _epi_glu(cv, cg, cs):
    return cv * _sigmoid(cg), _silu(cs)


def _epi_silu(z):
    return (_silu(z),)


def _epi_id(v):
    return (v,)


def _epi_gate(v, b):
    return (_sigmoid(v + b),)


def _dt_kernel(h_ref, w_ref, wt_ref, b_ref, bt_ref, dt_ref, dtt_ref):
    h = h_ref[...]
    dt_ref[...] = _softplus(_dot(h, w_ref[...]) + b_ref[...])
    raw_t = lax.dot_general(wt_ref[...], h, (((1,), (1,)), ((), ())), preferred_element_type=F32)
    dtt_ref[...] = _softplus(raw_t + bt_ref[...])


def _dt_call(h, w_dt, w_dt_t, bias, tm=1024):
    s, d = h.shape
    n = w_dt.shape[1]
    return pl.pallas_call(
        _dt_kernel,
        grid=(s // tm,),
        in_specs=[pl.BlockSpec((tm, d), lambda i: (i, 0)),
                  pl.BlockSpec((d, n), lambda i: (0, 0)),
                  pl.BlockSpec((n, d), lambda i: (0, 0)),
                  pl.BlockSpec((1, n), lambda i: (0, 0)),
                  pl.BlockSpec((n, 1), lambda i: (0, 0))],
        out_specs=[pl.BlockSpec((tm, n), lambda i: (i, 0)), pl.BlockSpec((n, tm), lambda i: (0, i))],
        out_shape=[jax.ShapeDtypeStruct((s, n), F32), jax.ShapeDtypeStruct((n, s), F32)],
        compiler_params=_cparams("parallel"),
        name="dt_proj",
    )(h, w_dt, w_dt_t, bias.reshape(1, n), bias.reshape(n, 1))


def _fill_halo_buf(buf_ref, prev_ref, cur_ref, next_ref, ts):
    i = pl.program_id(0)
    zeros = jnp.zeros((HALO, buf_ref.shape[1]), F32)
    buf_ref[pl.ds(HALO, ts), :] = cur_ref[...].astype(F32)

    @pl.when(i == 0)
    def _():
        buf_ref[pl.ds(0, HALO), :] = zeros

    @pl.when(i > 0)
    def _():
        buf_ref[pl.ds(0, HALO), :] = prev_ref[...].astype(F32)

    @pl.when(i == pl.num_programs(0) - 1)
    def _():
        buf_ref[pl.ds(HALO + ts, HALO), :] = zeros

    @pl.when(i < pl.num_programs(0) - 1)
    def _():
        buf_ref[pl.ds(HALO + ts, HALO), :] = next_ref[...].astype(F32)


def _dwconv_block(buf_ref, w_ref, b_ref, lanes, r0, rows, k_taps):
    pad = k_taps // 2
    acc = jnp.broadcast_to(b_ref[:, lanes], (rows, LANES))
    for k in range(k_taps):
        acc = acc + w_ref[k:k + 1, lanes] * buf_ref[pl.ds(HALO - pad + r0 + k, rows), lanes]
    return acc


def _halo_specs(s, ts, c):
    n = ts // HALO
    last = s // HALO - 1
    return [
        pl.BlockSpec((HALO, c), lambda i: (jnp.maximum(i * n - 1, 0), 0)),
        pl.BlockSpec((ts, c), lambda i: (i, 0)),
        pl.BlockSpec((HALO, c), lambda i: (jnp.minimum((i + 1) * n, last), 0)),
    ]


CONV_ROWS = 64


def _conv_kernel(prev_ref, cur_ref, next_ref, sg_ref, w_ref, b_ref, g_ref, beta_ref, o_ref, buf_ref, c_ref, *, ts):
    _fill_halo_buf(buf_ref, prev_ref, cur_ref, next_ref, ts)
    n_lane_blocks = CONV_DIM // LANES

    def lane_body(cb, carry):
        lanes = pl.ds(pl.multiple_of(cb * LANES, LANES), LANES)
        for rb in range(ts // CONV_ROWS):
            c_ref[pl.ds(rb * CONV_ROWS, CONV_ROWS), lanes] = _dwconv_block(
                buf_ref, w_ref, b_ref, lanes, rb * CONV_ROWS, CONV_ROWS, CONV_K)
        return carry

    lax.fori_loop(0, n_lane_blocks, lane_body, 0)

    def row_body(rb, carry):
        rows = pl.ds(pl.multiple_of(rb * CONV_ROWS, CONV_ROWS), CONV_ROWS)
        c = c_ref[rows, :]
        mu = jnp.mean(c, axis=-1, keepdims=True)
        xc = c - mu
        var = jnp.mean(xc * xc, axis=-1, keepdims=True)
        y = xc * lax.rsqrt(var + EPS) * g_ref[...] + beta_ref[...]
        o_ref[rows, :] = (_silu(y) * sg_ref[rows, :].astype(F32)).astype(o_ref.dtype)
        return carry

    lax.fori_loop(0, ts // CONV_ROWS, row_body, 0)


def _conv_call(u, sg, dw_w, dw_b, ln_g, ln_b, ts=512):
    s, c = u.shape
    row = lambda v: v.reshape(1, c)
    const = lambda shape: pl.BlockSpec(shape, lambda i: (0, 0))
    return pl.pallas_call(
        functools.partial(_conv_kernel, ts=ts),
        grid=(s // ts,),
        in_specs=_halo_specs(s, ts, c) + [pl.BlockSpec((ts, c), lambda i: (i, 0)),
                                       const((CONV_K, c)), const((1, c)), const((1, c)), const((1, c))],
        out_specs=pl.BlockSpec((ts, c), lambda i: (i, 0)),
        out_shape=jax.ShapeDtypeStruct((s, c), BF16),
        scratch_shapes=[pltpu.VMEM((ts + 2 * HALO, c), F32), pltpu.VMEM((ts, c), F32)],
        compiler_params=_cparams("arbitrary"),
        name="conv_branch",
    )(u, u, u, sg, dw_w, row(dw_b), row(ln_g), row(ln_b))


def _sconv_kernel(prev_ref, cur_ref, next_ref, w_ref, b_ref, o_ref, buf_ref, *, ts):
    _fill_halo_buf(buf_ref, prev_ref, cur_ref, next_ref, ts)

    def lane_body(cb, carry):
        lanes = pl.ds(pl.multiple_of(cb * LANES, LANES), LANES)
        for rb in range(ts // CONV_ROWS):
            acc = _dwconv_block(buf_ref, w_ref, b_ref, lanes, rb * CONV_ROWS, CONV_ROWS, SSM_CONV_K)
            o_ref[pl.ds(rb * CONV_ROWS, CONV_ROWS), lanes] = _silu(acc).astype(o_ref.dtype)
        return carry

    lax.fori_loop(0, XBC_DIM // LANES, lane_body, 0)


def _sconv_call(xbc, w, b, ts=256):
    s, c = xbc.shape
    const = lambda shape: pl.BlockSpec(shape, lambda i: (0, 0))
    return pl.pallas_call(
        functools.partial(_sconv_kernel, ts=ts),
        grid=(s // ts,),
        in_specs=_halo_specs(s, ts, c) + [const((SSM_CONV_K, c)), const((1, c))],
        out_specs=pl.BlockSpec((ts, c), lambda i: (i, 0)),
        out_shape=jax.ShapeDtypeStruct((s, c), BF16),
        scratch_shapes=[pltpu.VMEM((ts + 2 * HALO, c), F32)],
        compiler_params=_cparams("arbitrary"),
        name="ssd_conv",
    )(xbc, xbc, xbc, w, b.reshape(1, c))


def _scan_kernel(x_ref, b_ref, c_ref, dt_ref, dtt_ref, alog_ref, alogt_ref, tri_ref, trit_ref, e_ref, *rest,
                 reverse, final):
    if final:
        yb_ref, z_ref, dskip_ref, nw_ref, o_ref, state_ref, y_scr = rest
    else:
        o_ref, state_ref, y_scr = rest
    L = SCAN_L
    off = N_HEADS if reverse else 0
    last = 0 if reverse else L - 1

    @pl.when(pl.program_id(0) == 0)
    def _():
        state_ref[...] = jnp.zeros_like(state_ref)

    dt = dt_ref[...]
    dtt = dtt_ref[...]
    a = dt * (-jnp.exp(alog_ref[...]))
    at = dtt * (-jnp.exp(alogt_ref[...]))
    tri = tri_ref[...]
    trit = trit_ref[...]
    cs = sum(_dot(tri, p) for p in _split3(a))
    cst = sum(_dot(p, trit) for p in _split3(at))
    cst_m = cst - jnp.log(dtt)

    cs_last = cs[last:last + 1, :]
    e = e_ref[...]
    expand = lambda v: sum(_dot(p, e) for p in _split3(v)[:2])
    expcs_rep = expand(jnp.exp(cs))
    w_rep = expand(jnp.exp(cs_last - cs) * dt)
    explast_rep = expcs_rep[last:last + 1, :]

    li = lax.broadcasted_iota(jnp.int32, (L, L), 0)
    si = lax.broadcasted_iota(jnp.int32, (L, L), 1)
    mask = (si >= li) if reverse else (si <= li)
    lane = lax.broadcasted_iota(jnp.int32, (L, LANES), 1)
    lo_half = lane < HEAD_DIM

    x = x_ref[...]
    xw = (x.astype(F32) * w_rep).astype(BF16)

    for g in range(N_GROUPS):
        bg = b_ref[:, g * D_STATE:(g + 1) * D_STATE]
        cg = c_ref[:, g * D_STATE:(g + 1) * D_STATE]
        cb = lax.dot_general(cg, bg, (((1,), (1,)), ((), ())), preferred_element_type=F32)
        for hp in range(HEADS_PER_GROUP // 2):
            ms = []
            for h in (g * HEADS_PER_GROUP + 2 * hp, g * HEADS_PER_GROUP + 2 * hp + 1):
                seg = cs[:, off + h:off + h + 1] - cst_m[off + h:off + h + 1, :]
                ms.append((cb * jnp.exp(jnp.where(mask, seg, -1e30))).astype(BF16))
            col = (g * HEADS_PER_GROUP + 2 * hp) * HEAD_DIM
            xp = x[:, col:col + LANES]
            zero = jnp.zeros_like(xp)
            rhs = jnp.concatenate([jnp.where(lo_half, xp, zero), jnp.where(lo_half, zero, xp)], axis=0)
            y_scr[:, col:col + LANES] = _dot(jnp.concatenate(ms, axis=1), rhs)
        gs = slice(g * GROUP_W, (g + 1) * GROUP_W)
        st = state_ref[g]
        y_scr[:, gs] = y_scr[:, gs] + _dot(cg, st.astype(BF16)) * expcs_rep[:, gs]
        upd = lax.dot_general(bg, xw[:, gs], (((0,), (0,)), ((), ())), preferred_element_type=F32)
        state_ref[g] = st * explast_rep[:, gs] + upd

    if not final:
        o_ref[...] = y_scr[...].astype(o_ref.dtype)
    else:
        y = y_scr[...] + yb_ref[...].astype(F32) + x.astype(F32) * dskip_ref[...]
        y = y * z_ref[...].astype(F32)
        for g in range(N_GROUPS):
            gs = slice(g * NORM_GROUP, (g + 1) * NORM_GROUP)
            yg = y[:, gs]
            ms = jnp.mean(yg * yg, axis=-1, keepdims=True)
            o_ref[:, gs] = (yg * lax.rsqrt(ms + EPS) * nw_ref[:, gs]).astype(o_ref.dtype)


def _scan_consts(reverse):
    l = np.arange(SCAN_L)
    tri = (l[None, :] >= l[:, None]) if reverse else (l[None, :] <= l[:, None])
    off = N_HEADS if reverse else 0
    e = np.zeros((2 * N_HEADS, SSM_DIM), np.float32)
    for h in range(N_HEADS):
        e[off + h, h * HEAD_DIM:(h + 1) * HEAD_DIM] = 1.0
    return (jnp.asarray(tri, dtype=BF16), jnp.asarray(tri.T, dtype=BF16), jnp.asarray(e, dtype=BF16))


def _scan_call(xbc, dt, dtt, a_log, reverse, final_inputs=None):
    s = xbc.shape[0]
    nc = s // SCAN_L
    L = SCAN_L
    final = final_inputs is not None
    ch = (lambda t: nc - 1 - t) if reverse else (lambda t: t)
    tri, trit, e = _scan_consts(reverse)
    alog = a_log.reshape(1, 2 * N_HEADS)
    const = lambda shape: pl.BlockSpec(shape, lambda t: (0, 0))
    x_blocks = SSM_DIM // BC_DIM
    in_specs = [
        pl.BlockSpec((L, SSM_DIM), lambda t: (ch(t), 0)),
        pl.BlockSpec((L, BC_DIM), lambda t: (ch(t), x_blocks)),
        pl.BlockSpec((L, BC_DIM), lambda t: (ch(t), x_blocks + 1)),
        pl.BlockSpec((L, 2 * N_HEADS), lambda t: (ch(t), 0)),
        pl.BlockSpec((2 * N_HEADS, L), lambda t: (0, ch(t))),
        const((1, 2 * N_HEADS)), const((2 * N_HEADS, 1)),
        const((L, L)), const((L, L)), const((2 * N_HEADS, SSM_DIM)),
    ]
    args = [xbc, xbc, xbc, dt, dtt, alog, alog.reshape(2 * N_HEADS, 1), tri, trit, e]
    if final:
        yb, sz, d_skip, norm_w = final_inputs
        in_specs += [pl.BlockSpec((L, SSM_DIM), lambda t: (ch(t), 0)),
                     pl.BlockSpec((L, SSM_DIM), lambda t: (ch(t), 0)),
                     const((1, SSM_DIM)), const((1, SSM_DIM))]
        args += [yb, sz, jnp.repeat(d_skip, HEAD_DIM).reshape(1, SSM_DIM), norm_w.reshape(1, SSM_DIM)]
    return pl.pallas_call(
        functools.partial(_scan_kernel, reverse=reverse, final=final),
        grid=(nc,),
        in_specs=in_specs,
        out_specs=pl.BlockSpec((L, SSM_DIM), lambda t: (ch(t), 0)),
        out_shape=jax.ShapeDtypeStruct((s, SSM_DIM), BF16),
        scratch_shapes=[pltpu.VMEM((N_GROUPS, D_STATE, GROUP_W), F32), pltpu.VMEM((L, SSM_DIM), F32)],
        compiler_params=_cparams("arbitrary"),
        name="ssd_scan_bwd" if reverse else "ssd_scan_fwd",
    )(*args)


def _merge_kernel(yc_ref, ys_ref, gc_ref, gs_ref, wc_ref, ws_ref, o_ref):
    oc = _dot(yc_ref[...], wc_ref[...])
    os_ = _dot(ys_ref[...], ws_ref[...])
    o_ref[...] = (gc_ref[...].astype(F32) * oc + gs_ref[...].astype(F32) * os_).astype(o_ref.dtype)


def _merge_call(yc, ys, gate, wc, ws, tm=1024, tn=512):
    s = yc.shape[0]
    gate_blocks = D_MODEL // tn
    return pl.pallas_call(
        _merge_kernel,
        grid=(s // tm, D_MODEL // tn),
        in_specs=[pl.BlockSpec((tm, CONV_DIM), lambda i, j: (i, 0)),
                  pl.BlockSpec((tm, SSM_DIM), lambda i, j: (i, 0)),
                  pl.BlockSpec((tm, tn), lambda i, j: (i, j)),
                  pl.BlockSpec((tm, tn), lambda i, j: (i, j + gate_blocks)),
                  pl.BlockSpec((CONV_DIM, tn), lambda i, j: (0, j)),
                  pl.BlockSpec((SSM_DIM, tn), lambda i, j: (0, j))],
        out_specs=pl.BlockSpec((tm, tn), lambda i, j: (i, j)),
        out_shape=jax.ShapeDtypeStruct((s, D_MODEL), BF16),
        compiler_params=_cparams("parallel", "arbitrary"),
        name="merge",
    )(yc, ys, gate, gate, wc, ws)


def _out_kernel(x_ref, m_ref, w_ref, nw_ref, o_ref):
    r = x_ref[...] + _dot(m_ref[...], w_ref[...])
    ms = jnp.mean(r * r, axis=-1, keepdims=True)
    o_ref[...] = r * lax.rsqrt(ms + EPS) * nw_ref[...]


def _out_call(x, m, w_out, norm_w, tm=512):
    s, d = x.shape
    return pl.pallas_call(
        _out_kernel,
        grid=(s // tm,),
        in_specs=[pl.BlockSpec((tm, d), lambda i: (i, 0)),
                  pl.BlockSpec((tm, d), lambda i: (i, 0)),
                  pl.BlockSpec((d, d), lambda i: (0, 0)),
                  pl.BlockSpec((1, d), lambda i: (0, 0))],
        out_specs=pl.BlockSpec((tm, d), lambda i: (i, 0)),
        out_shape=jax.ShapeDtypeStruct((s, d), F32),
        compiler_params=_cparams("parallel"),
        name="out_proj",
    )(x, m, w_out, norm_w.reshape(1, d))


def _layer(x, p):
    h = _rms_call(x, p["norm_w"])
    u, sg = _proj_call(h, [p["w_cv"], p["w_cg"], p["w_cs"]], [], _epi_glu, 2, "proj_conv")
    (sz,) = _proj_call(h, [p["w_z"]], [], _epi_silu, 1, "proj_z")
    (xbc,) = _proj_call(h, [p["w_xbc"]], [], _epi_id, 1, "proj_xbc")
    (gate,) = _proj_call(h, [p["w_gate"]], [p["b_gate"]], _epi_gate, 1, "proj_gate")
    dt, dtt = _dt_call(h, p["w_dt"], p["w_dt_t"], p["dt_bias"])

    y_c = _conv_call(u, sg, p["dw_w"], p["dw_b"], p["ln_g"], p["ln_b"])

    xbc_act = _sconv_call(xbc, p["sconv_w"], p["sconv_b"])
    y_b = _scan_call(xbc_act, dt, dtt, p["a_log"], reverse=True)
    y_s = _scan_call(xbc_act, dt, dtt, p["a_log"], reverse=False,
                     final_inputs=(y_b, sz, p["d_skip"], p["ssm_norm_w"]))

    return _merge_call(y_c, y_s, gate, p["w_bc"], p["w_bs"])


def kernel(x_prompt, x_sample, norm_w, w_in, b_gate, dw_w, dw_b, ln_g, ln_b, sconv_w, sconv_b,
           dt_bias, a_log, d_skip, ssm_norm_w, w_branch, w_out, final_norm_w):
    depth = norm_w.shape[0]
    assert depth == 1, "the final norm is fused into the single layer's output projection"
    w = w_in[0]
    seg = lambda lo, hi: w[:, lo:hi].astype(BF16)
    p = dict(norm_w=norm_w[0], b_gate=b_gate[0].reshape(1, -1), dw_w=dw_w[0], dw_b=dw_b[0],
             ln_g=ln_g[0], ln_b=ln_b[0], sconv_w=sconv_w[0], sconv_b=sconv_b[0], dt_bias=dt_bias[0].reshape(-1),
             a_log=a_log[0], d_skip=d_skip[0], ssm_norm_w=ssm_norm_w[0],
             w_cv=seg(OFF_CV, OFF_CG), w_cg=seg(OFF_CG, OFF_CS), w_cs=seg(OFF_CS, OFF_Z), w_z=seg(OFF_Z, OFF_XBC),
             w_xbc=seg(OFF_XBC, OFF_DT), w_dt=seg(OFF_DT, OFF_GATE), w_dt_t=seg(OFF_DT, OFF_GATE).T,
             w_gate=seg(OFF_GATE, IN_DIM),
             w_bc=w_branch[0, :CONV_DIM].astype(BF16), w_bs=w_branch[0, CONV_DIM:].astype(BF16))
    w_out_b = w_out[0].astype(BF16)

    def trunk(x):
        bsz, s, d = x.shape
        outs = [_out_call(x[b], _layer(x[b], p), w_out_b, final_norm_w) for b in range(bsz)]
        return outs[0].reshape(1, s, d) if bsz == 1 else jnp.stack(outs, axis=0)

    return trunk(x_prompt), trunk(x_sample)
```

```python
import functools
import math

import jax
import jax.numpy as jnp
import numpy as np
from jax import lax
from jax.experimental import pallas as pl
from jax.experimental.pallas import tpu as pltpu

F32 = jnp.float32
BF16 = jnp.bfloat16

D_MODEL = 2048
CONV_DIM = D_MODEL
CONV_K = 31
SSM_DIM = 2 * D_MODEL
HEAD_DIM = 64
N_HEADS = SSM_DIM // HEAD_DIM
N_GROUPS = 8
HEADS_PER_GROUP = N_HEADS // N_GROUPS
D_STATE = 128
SSM_CONV_K = 5
BC_DIM = N_GROUPS * D_STATE
XBC_DIM = SSM_DIM + 2 * BC_DIM
NORM_GROUP = SSM_DIM // N_GROUPS
GROUP_W = HEADS_PER_GROUP * HEAD_DIM
EPS = 1e-5
OFF_CV = 0
OFF_CG = OFF_CV + CONV_DIM
OFF_CS = OFF_CG + CONV_DIM
OFF_Z = OFF_CS + CONV_DIM
OFF_XBC = OFF_Z + SSM_DIM
OFF_DT = OFF_XBC + XBC_DIM
OFF_GATE = OFF_DT + 2 * N_HEADS
IN_DIM = OFF_GATE + 2 * D_MODEL

LANES = 128
HALO = 16
SCAN_L = 128
LOG2E = math.log2(math.e)
EXP2_CLAMP = 127.0
VMEM_LIMIT = 56 * 1024 * 1024


def _cparams(*sem):
    return pltpu.CompilerParams(dimension_semantics=sem, vmem_limit_bytes=VMEM_LIMIT)


def _sigmoid(v):
    return 1.0 / (1.0 + jnp.exp(-v))


def _silu(v):
    return v * _sigmoid(v)


def _softplus(v):
    return jnp.maximum(v, 0.0) + jnp.log(1.0 + jnp.exp(-jnp.abs(v)))


def _split3(v):
    a1 = v.astype(BF16)
    r1 = v - a1.astype(F32)
    a2 = r1.astype(BF16)
    a3 = (r1 - a2.astype(F32)).astype(BF16)
    return a1, a2, a3


def _dot(a, b):
    return jnp.dot(a, b, preferred_element_type=F32)


def _dot_nt(a, b):
    return lax.dot_general(a, b, (((1,), (1,)), ((), ())), preferred_element_type=F32)


def _dot_tn(a, b):
    return lax.dot_general(a, b, (((0,), (0,)), ((), ())), preferred_element_type=F32)


def _rms_kernel(x_ref, w_ref, o_ref):
    x = x_ref[...]
    ms = jnp.mean(x * x, axis=-1, keepdims=True)
    o_ref[...] = (x * lax.rsqrt(ms + EPS) * w_ref[...]).astype(o_ref.dtype)


def _rms_call(x, w, tm=512):
    s, d = x.shape
    tm = min(tm, s)
    return pl.pallas_call(
        _rms_kernel,
        grid=(s // tm,),
        in_specs=[pl.BlockSpec((tm, d), lambda i: (i, 0)), pl.BlockSpec((1, d), lambda i: (0, 0))],
        out_specs=pl.BlockSpec((tm, d), lambda i: (i, 0)),
        out_shape=jax.ShapeDtypeStruct((s, d), BF16),
        compiler_params=_cparams("parallel"),
        name="rms_in",
    )(x, w.reshape(1, d))


def _skew_maps(ni, nj):
    n = ni * nj
    mm = lambda t: jnp.minimum(t, n - 1)
    ep = lambda t: jnp.maximum(t - 1, 0)
    return n, mm, ep


def _skew_step(t, step, accs_a, accs_b):
    @pl.when(t == 0)
    def _():
        for a in accs_b:
            a[...] = jnp.zeros_like(a)

    @pl.when(t % 2 == 0)
    def _():
        step(accs_a, accs_b)

    @pl.when(t % 2 == 1)
    def _():
        step(accs_b, accs_a)


def _proj_kernel(h_ref, *refs, n_w, n_extra, n_out, epilogue):
    w_refs = refs[:n_w]
    extra = refs[n_w:n_w + n_extra]
    out_refs = refs[n_w + n_extra:n_w + n_extra + n_out]
    accs = refs[n_w + n_extra + n_out:]

    def step(acc_w, acc_r):
        outs = epilogue(*[a[...] for a in acc_r], *[e[...] for e in extra])
        for o_ref, o in zip(out_refs, outs):
            o_ref[...] = o.astype(o_ref.dtype)
        h = h_ref[...]
        for a, w in zip(acc_w, w_refs):
            a[...] = _dot(h, w[...])

    _skew_step(pl.program_id(0), step, accs[:n_w], accs[n_w:])


def _proj_call(h, ws, extras, epilogue, n_out, name, tm=1024, tn=512):
    s, d = h.shape
    tm = min(tm, s)
    n = ws[0].shape[1]
    nj = n // tn
    nblk, mm, ep = _skew_maps(s // tm, nj)
    kern = functools.partial(_proj_kernel, n_w=len(ws), n_extra=len(extras), n_out=n_out, epilogue=epilogue)
    return pl.pallas_call(
        kern,
        grid=(nblk + 1,),
        in_specs=([pl.BlockSpec((tm, d), lambda t: (mm(t) // nj, 0))]
                  + [pl.BlockSpec((d, tn), lambda t: (0, mm(t) % nj)) for _ in ws]
                  + [pl.BlockSpec((1, tn), lambda t: (0, ep(t) % nj)) for _ in extras]),
        out_specs=[pl.BlockSpec((tm, tn), lambda t: (ep(t) // nj, ep(t) % nj)) for _ in range(n_out)],
        out_shape=[jax.ShapeDtypeStruct((s, n), BF16) for _ in range(n_out)],
        scratch_shapes=[pltpu.VMEM((tm, tn), F32) for _ in range(2 * len(ws))],
        compiler_params=_cparams("arbitrary"),
        name=name,
    )(h, *ws, *extras)


def _epi_glu(cv, cg, cs):
    return cv * _sigmoid(cg), _silu(cs)


def _epi_silu(z):
    return (_silu(z),)


def _epi_gate(v, b):
    return (_sigmoid(v + b),)


def _dt_kernel(h_ref, w_ref, wt_ref, b_ref, bt_ref, dt_ref, dtt_ref):
    h = h_ref[...]
    dt_ref[...] = _softplus(_dot(h, w_ref[...]) + b_ref[...])
    dtt_ref[...] = _softplus(_dot_nt(wt_ref[...], h) + bt_ref[...])


def _dt_call(h, w_dt, w_dt_t, bias, tm=1024):
    s, d = h.shape
    tm = min(tm, s)
    n = w_dt.shape[1]
    return pl.pallas_call(
        _dt_kernel,
        grid=(s // tm,),
        in_specs=[pl.BlockSpec((tm, d), lambda i: (i, 0)),
                  pl.BlockSpec((d, n), lambda i: (0, 0)),
                  pl.BlockSpec((n, d), lambda i: (0, 0)),
                  pl.BlockSpec((1, n), lambda i: (0, 0)),
                  pl.BlockSpec((n, 1), lambda i: (0, 0))],
        out_specs=[pl.BlockSpec((tm, n), lambda i: (i, 0)), pl.BlockSpec((n, tm), lambda i: (0, i))],
        out_shape=[jax.ShapeDtypeStruct((s, n), F32), jax.ShapeDtypeStruct((n, s), F32)],
        compiler_params=_cparams("parallel"),
        name="dt_proj",
    )(h, w_dt, w_dt_t, bias.reshape(1, n), bias.reshape(n, 1))


def _halo_specs(s, ts, c, tile_of=lambda t: t):
    n = ts // HALO
    last = s // HALO - 1
    return [pl.BlockSpec((HALO, c), lambda t: (jnp.maximum(tile_of(t) * n - 1, 0), 0)),
            pl.BlockSpec((ts, c), lambda t: (tile_of(t), 0)),
            pl.BlockSpec((HALO, c), lambda t: (jnp.minimum((tile_of(t) + 1) * n, last), 0))]


def _fill_halo_buf(buf_ref, prev_ref, cur_ref, next_ref, ts, i, n_tiles):
    dt = buf_ref.dtype
    zeros = jnp.zeros((HALO, buf_ref.shape[1]), dt)
    buf_ref[pl.ds(HALO, ts), :] = cur_ref[...].astype(dt)

    @pl.when(i == 0)
    def _():
        buf_ref[pl.ds(0, HALO), :] = zeros

    @pl.when(i > 0)
    def _():
        buf_ref[pl.ds(0, HALO), :] = prev_ref[...].astype(dt)

    @pl.when(i == n_tiles - 1)
    def _():
        buf_ref[pl.ds(HALO + ts, HALO), :] = zeros

    @pl.when(i < n_tiles - 1)
    def _():
        buf_ref[pl.ds(HALO + ts, HALO), :] = next_ref[...].astype(dt)


SUBLANES = 8


def _load_rows(ref, start, rows, lanes):
    s = start % SUBLANES
    if s == 0:
        return ref[pl.ds(start, rows), lanes]
    base = start - s
    shape3 = (rows // SUBLANES, SUBLANES, LANES)
    lo = ref[pl.ds(base, rows), lanes].reshape(shape3)
    hi = ref[pl.ds(base + SUBLANES, rows), lanes].reshape(shape3)
    sub = lax.broadcasted_iota(jnp.int32, (1, SUBLANES, LANES), 1)
    mixed = jnp.where(sub >= s, lo, hi)
    return pltpu.roll(mixed, SUBLANES - s, 1).reshape(rows, LANES)


def _dwconv_block(buf_ref, w_ref, b_ref, lanes, r0, rows, k_taps):
    pad = k_taps // 2
    acc = jnp.broadcast_to(b_ref[:, lanes], (rows, LANES))
    for k in range(k_taps):
        acc = acc + w_ref[k:k + 1, lanes] * _load_rows(buf_ref, HALO - pad + r0 + k, rows, lanes)
    return acc


SCONV_ROWS = 32


def _proj_sconv_kernel(hp_ref, hc_ref, hn_ref, w_ref, cw_ref, cb_ref, o_ref, hcat_ref, acc_a, acc_b,
                       *, tm, tn, ni, nj):
    t = pl.program_id(0)

    @pl.when(jnp.logical_and(t % nj == 0, t < ni * nj))
    def _():
        _fill_halo_buf(hcat_ref, hp_ref, hc_ref, hn_ref, tm, t // nj, ni)

    def step(acc_w, acc_r):
        for rb in range(tm // SCONV_ROWS):
            for lb in range(tn // LANES):
                lanes = pl.ds(lb * LANES, LANES)
                acc = _dwconv_block(acc_r[0], cw_ref, cb_ref, lanes, rb * SCONV_ROWS, SCONV_ROWS, SSM_CONV_K)
                o_ref[pl.ds(rb * SCONV_ROWS, SCONV_ROWS), lanes] = _silu(acc).astype(o_ref.dtype)
        acc_w[0][...] = _dot(hcat_ref[...], w_ref[...])

    _skew_step(t, step, [acc_a], [acc_b])


def _proj_sconv_call(h, w, cw, cb, tm=1024, tn=512):
    s, d = h.shape
    tm = min(tm, s)
    n = w.shape[1]
    ni, nj = s // tm, n // tn
    nblk, mm, ep = _skew_maps(ni, nj)
    return pl.pallas_call(
        functools.partial(_proj_sconv_kernel, tm=tm, tn=tn, ni=ni, nj=nj),
        grid=(nblk + 1,),
        in_specs=_halo_specs(s, tm, d, tile_of=lambda t: mm(t) // nj) + [
            pl.BlockSpec((d, tn), lambda t: (0, mm(t) % nj)),
            pl.BlockSpec((SSM_CONV_K, tn), lambda t: (0, ep(t) % nj)),
            pl.BlockSpec((1, tn), lambda t: (0, ep(t) % nj))],
        out_specs=pl.BlockSpec((tm, tn), lambda t: (ep(t) // nj, ep(t) % nj)),
        out_shape=jax.ShapeDtypeStruct((s, n), BF16),
        scratch_shapes=[pltpu.VMEM((tm + 2 * HALO, d), BF16),
                        pltpu.VMEM((tm + 2 * HALO, tn), F32), pltpu.VMEM((tm + 2 * HALO, tn), F32)],
        compiler_params=_cparams("arbitrary"),
        name="proj_xbc_conv",
    )(h, h, h, w, cw, cb.reshape(1, n))


CONV_ROWS = 64


def _conv_kernel(prev_ref, cur_ref, next_ref, sg_ref, w_ref, b_ref, g_ref, beta_ref, o_ref, buf_ref, c_ref, *, ts):
    _fill_halo_buf(buf_ref, prev_ref, cur_ref, next_ref, ts, pl.program_id(0), pl.num_programs(0))
    n_lane_blocks = CONV_DIM // LANES

    def lane_body(cb, carry):
        lanes = pl.ds(pl.multiple_of(cb * LANES, LANES), LANES)
        for rb in range(ts // CONV_ROWS):
            c_ref[pl.ds(rb * CONV_ROWS, CONV_ROWS), lanes] = _dwconv_block(
                buf_ref, w_ref, b_ref, lanes, rb * CONV_ROWS, CONV_ROWS, CONV_K)
        return carry

    lax.fori_loop(0, n_lane_blocks, lane_body, 0)

    def row_body(rb, carry):
        rows = pl.ds(pl.multiple_of(rb * CONV_ROWS, CONV_ROWS), CONV_ROWS)
        c = c_ref[rows, :]
        mu = jnp.mean(c, axis=-1, keepdims=True)
        xc = c - mu
        var = jnp.mean(xc * xc, axis=-1, keepdims=True)
        y = xc * lax.rsqrt(var + EPS) * g_ref[...] + beta_ref[...]
        o_ref[rows, :] = (_silu(y) * sg_ref[rows, :].astype(F32)).astype(o_ref.dtype)
        return carry

    lax.fori_loop(0, ts // CONV_ROWS, row_body, 0)


def _conv_call(u, sg, dw_w, dw_b, ln_g, ln_b, ts=512):
    s, c = u.shape
    ts = min(ts, s)
    row = lambda v: v.reshape(1, c)
    const = lambda shape: pl.BlockSpec(shape, lambda i: (0, 0))
    return pl.pallas_call(
        functools.partial(_conv_kernel, ts=ts),
        grid=(s // ts,),
        in_specs=_halo_specs(s, ts, c) + [pl.BlockSpec((ts, c), lambda i: (i, 0)),
                                          const((CONV_K, c)), const((1, c)), const((1, c)), const((1, c))],
        out_specs=pl.BlockSpec((ts, c), lambda i: (i, 0)),
        out_shape=jax.ShapeDtypeStruct((s, c), BF16),
        scratch_shapes=[pltpu.VMEM((ts + 2 * HALO, c), F32), pltpu.VMEM((ts, c), F32)],
        compiler_params=_cparams("arbitrary"),
        name="conv_branch",
    )(u, u, u, sg, dw_w, row(dw_b), row(ln_g), row(ln_b))


def _scan_kernel(x_ref, b_ref, c_ref, dt_ref, dtt_ref, alog_ref, alogt_ref, tri_ref, trit_ref, e_ref,
                 o_ref, state_ref, cb_ref, yi_ref, cs_ref, cstm_ref, prep_ref, wrep_ref, *, reverse):
    L = SCAN_L
    nh2 = 2 * N_HEADS
    off = N_HEADS if reverse else 0
    last = 0 if reverse else L - 1

    @pl.when(pl.program_id(0) == 0)
    def _():
        state_ref[...] = jnp.zeros_like(state_ref)

    tri_f = tri_ref[...].astype(F32)
    for g in range(N_GROUPS):
        bg = b_ref[:, g * D_STATE:(g + 1) * D_STATE]
        cg = c_ref[:, g * D_STATE:(g + 1) * D_STATE]
        cb_ref[g] = _dot_nt(cg, bg) * tri_f
        yi_ref[:, g * GROUP_W:(g + 1) * GROUP_W] = _dot(cg, state_ref[g].astype(BF16))

    dt = dt_ref[...]
    dtt = dtt_ref[...]
    a = dt * (-jnp.exp(alog_ref[...]) * LOG2E)
    at = dtt * (-jnp.exp(alogt_ref[...]) * LOG2E)
    p = _dot(tri_ref[...], jnp.concatenate(_split3(a), axis=1))
    cs = p[:, :nh2] + p[:, nh2:2 * nh2] + p[:, 2 * nh2:]
    pt = _dot(jnp.concatenate(_split3(at), axis=0), trit_ref[...])
    cst = pt[:nh2] + pt[nh2:2 * nh2] + pt[2 * nh2:]
    cs_ref[...] = cs
    cstm_ref[...] = cst - jnp.log(dtt) * LOG2E

    cs_last = cs[last:last + 1, :]
    pcs = jnp.exp2(cs)
    w = jnp.exp2(cs_last - cs) * dt
    both = jnp.concatenate([pcs, w], axis=0)
    hi = both.astype(BF16)
    lo = (both - hi.astype(F32)).astype(BF16)
    rep = _dot(jnp.concatenate([hi, lo], axis=1), e_ref[...])
    prep_ref[...] = rep[:L]
    wrep_ref[...] = rep[L:].astype(BF16)

    for g in range(N_GROUPS):
        gs = slice(g * GROUP_W, (g + 1) * GROUP_W)
        xw = x_ref[:, gs] * wrep_ref[:, gs]
        upd = _dot_tn(b_ref[:, g * D_STATE:(g + 1) * D_STATE], xw)
        state_ref[g] = state_ref[g] * prep_ref[last:last + 1, gs] + upd

    lo_half = lax.broadcasted_iota(jnp.int32, (L, LANES), 1) < HEAD_DIM
    for g in range(N_GROUPS):
        for hp in range(HEADS_PER_GROUP // 2):
            h0 = g * HEADS_PER_GROUP + 2 * hp
            ms = []
            for h in (h0, h0 + 1):
                seg = cs_ref[:, off + h:off + h + 1] - cstm_ref[off + h:off + h + 1, :]
                ms.append((cb_ref[g] * jnp.exp2(jnp.minimum(seg, EXP2_CLAMP))).astype(BF16))
            cols = slice(h0 * HEAD_DIM, h0 * HEAD_DIM + LANES)
            xp = x_ref[:, cols]
            zero = jnp.zeros_like(xp)
            rhs = jnp.concatenate([jnp.where(lo_half, xp, zero), jnp.where(lo_half, zero, xp)], axis=0)
            y = _dot(jnp.concatenate(ms, axis=1), rhs) + yi_ref[:, cols] * prep_ref[:, cols]
            o_ref[:, cols] = y.astype(o_ref.dtype)


def _scan_consts(reverse):
    l = np.arange(SCAN_L)
    tri = (l[None, :] >= l[:, None]) if reverse else (l[None, :] <= l[:, None])
    off = N_HEADS if reverse else 0
    e = np.zeros((2, 2 * N_HEADS, SSM_DIM), np.float32)
    for h in range(N_HEADS):
        e[:, off + h, h * HEAD_DIM:(h + 1) * HEAD_DIM] = 1.0
    return (jnp.asarray(tri, dtype=BF16), jnp.asarray(tri.T, dtype=BF16),
            jnp.asarray(e.reshape(4 * N_HEADS, SSM_DIM), dtype=BF16))


def _scan_call(xbc, dt, dtt, a_log, reverse):
    s = xbc.shape[0]
    nc = s // SCAN_L
    L = SCAN_L
    ch = (lambda t: nc - 1 - t) if reverse else (lambda t: t)
    tri, trit, e = _scan_consts(reverse)
    alog = a_log.reshape(1, 2 * N_HEADS)
    const = lambda shape: pl.BlockSpec(shape, lambda t: (0, 0))
    x_blocks = SSM_DIM // BC_DIM
    in_specs = [
        pl.BlockSpec((L, SSM_DIM), lambda t: (ch(t), 0)),
        pl.BlockSpec((L, BC_DIM), lambda t: (ch(t), x_blocks)),
        pl.BlockSpec((L, BC_DIM), lambda t: (ch(t), x_blocks + 1)),
        pl.BlockSpec((L, 2 * N_HEADS), lambda t: (ch(t), 0)),
        pl.BlockSpec((2 * N_HEADS, L), lambda t: (0, ch(t))),
        const((1, 2 * N_HEADS)), const((2 * N_HEADS, 1)),
        const((L, L)), const((L, L)), const((4 * N_HEADS, SSM_DIM)),
    ]
    return pl.pallas_call(
        functools.partial(_scan_kernel, reverse=reverse),
        grid=(nc,),
        in_specs=in_specs,
        out_specs=pl.BlockSpec((L, SSM_DIM), lambda t: (ch(t), 0)),
        out_shape=jax.ShapeDtypeStruct((s, SSM_DIM), BF16),
        scratch_shapes=[pltpu.VMEM((N_GROUPS, D_STATE, GROUP_W), F32),
                        pltpu.VMEM((N_GROUPS, L, L), F32), pltpu.VMEM((L, SSM_DIM), F32),
                        pltpu.VMEM((L, 2 * N_HEADS), F32), pltpu.VMEM((2 * N_HEADS, L), F32),
                        pltpu.VMEM((L, SSM_DIM), F32), pltpu.VMEM((L, SSM_DIM), BF16)],
        compiler_params=_cparams("arbitrary"),
        name="ssd_scan_bwd" if reverse else "ssd_scan_fwd",
    )(xbc, xbc, xbc, dt, dtt, alog, alog.reshape(2 * N_HEADS, 1), tri, trit, e)


MERGE_KB = 1024


def _merge_kernel(yc_ref, yf_ref, yb_ref, x_ref, sz_ref, dskip_ref, nw_ref, w_ref, gc_ref, gs_ref, o_ref,
                  accc_ref, accs_ref, *, n_conv_steps, n_steps):
    k = pl.program_id(1)

    @pl.when(k == 0)
    def _():
        accc_ref[...] = _dot(yc_ref[...], w_ref[...])

    @pl.when(jnp.logical_and(k > 0, k < n_conv_steps))
    def _():
        accc_ref[...] += _dot(yc_ref[...], w_ref[...])

    def ssd_block():
        y = yf_ref[...].astype(F32) + yb_ref[...].astype(F32) + x_ref[...].astype(F32) * dskip_ref[...]
        y = y * sz_ref[...].astype(F32)
        parts = []
        for g in range(MERGE_KB // NORM_GROUP):
            yg = y[:, g * NORM_GROUP:(g + 1) * NORM_GROUP]
            ms = jnp.mean(yg * yg, axis=-1, keepdims=True)
            parts.append((yg * lax.rsqrt(ms + EPS) * nw_ref[:, g * NORM_GROUP:(g + 1) * NORM_GROUP]).astype(BF16))
        return _dot(jnp.concatenate(parts, axis=1), w_ref[...])

    @pl.when(k == n_conv_steps)
    def _():
        accs_ref[...] = ssd_block()

    @pl.when(k > n_conv_steps)
    def _():
        accs_ref[...] += ssd_block()

    @pl.when(k == n_steps - 1)
    def _():
        o_ref[...] = (gc_ref[...].astype(F32) * accc_ref[...]
                      + gs_ref[...].astype(F32) * accs_ref[...]).astype(o_ref.dtype)


def _merge_call(yc, yf, yb, xbc, sz, gate, d_skip_rep, norm_w, w_branch, tm=512):
    s = yc.shape[0]
    tm = min(tm, s)
    kb = MERGE_KB
    nc_steps = CONV_DIM // kb
    n_steps = nc_steps + SSM_DIM // kb
    conv_k = lambda i, k: (i, jnp.minimum(k, nc_steps - 1))
    ssd_k = lambda i, k: (i, jnp.maximum(k - nc_steps, 0))
    ssd_row = lambda i, k: (0, jnp.maximum(k - nc_steps, 0))
    return pl.pallas_call(
        functools.partial(_merge_kernel, n_conv_steps=nc_steps, n_steps=n_steps),
        grid=(s // tm, n_steps),
        in_specs=[pl.BlockSpec((tm, kb), conv_k),
                  pl.BlockSpec((tm, kb), ssd_k), pl.BlockSpec((tm, kb), ssd_k),
                  pl.BlockSpec((tm, kb), ssd_k), pl.BlockSpec((tm, kb), ssd_k),
                  pl.BlockSpec((1, kb), ssd_row), pl.BlockSpec((1, kb), ssd_row),
                  pl.BlockSpec((kb, D_MODEL), lambda i, k: (k, 0)),
                  pl.BlockSpec((tm, D_MODEL), lambda i, k: (i, 0)),
                  pl.BlockSpec((tm, D_MODEL), lambda i, k: (i, 1))],
        out_specs=pl.BlockSpec((tm, D_MODEL), lambda i, k: (i, 0)),
        out_shape=jax.ShapeDtypeStruct((s, D_MODEL), BF16),
        scratch_shapes=[pltpu.VMEM((tm, D_MODEL), F32), pltpu.VMEM((tm, D_MODEL), F32)],
        compiler_params=_cparams("parallel", "arbitrary"),
        name="merge",
    )(yc, yf, yb, xbc, sz, d_skip_rep, norm_w, w_branch, gate, gate)


def _out_kernel(x_ref, m_ref, w_ref, nw_ref, o_ref):
    r = x_ref[...] + _dot(m_ref[...], w_ref[...])
    ms = jnp.mean(r * r, axis=-1, keepdims=True)
    o_ref[...] = r * lax.rsqrt(ms + EPS) * nw_ref[...]


def _out_call(x, m, w_out, norm_w, tm=512):
    s, d = x.shape
    tm = min(tm, s)
    return pl.pallas_call(
        _out_kernel,
        grid=(s // tm,),
        in_specs=[pl.BlockSpec((tm, d), lambda i: (i, 0)),
                  pl.BlockSpec((tm, d), lambda i: (i, 0)),
                  pl.BlockSpec((d, d), lambda i: (0, 0)),
                  pl.BlockSpec((1, d), lambda i: (0, 0))],
        out_specs=pl.BlockSpec((tm, d), lambda i: (i, 0)),
        out_shape=jax.ShapeDtypeStruct((s, d), F32),
        compiler_params=_cparams("parallel"),
        name="out_proj",
    )(x, m, w_out, norm_w.reshape(1, d))


def _layer(x, p):
    h = _rms_call(x, p["norm_w"])
    u, sg = _proj_call(h, [p["w_cv"], p["w_cg"], p["w_cs"]], [], _epi_glu, 2, "proj_conv")
    (sz,) = _proj_call(h, [p["w_z"]], [], _epi_silu, 1, "proj_z")
    (gate,) = _proj_call(h, [p["w_gate"]], [p["b_gate"]], _epi_gate, 1, "proj_gate")
    dt, dtt = _dt_call(h, p["w_dt"], p["w_dt_t"], p["dt_bias"])
    xbc = _proj_sconv_call(h, p["w_xbc"], p["sconv_w"], p["sconv_b"])

    y_c = _conv_call(u, sg, p["dw_w"], p["dw_b"], p["ln_g"], p["ln_b"])
    y_b = _scan_call(xbc, dt, dtt, p["a_log"], reverse=True)
    y_f = _scan_call(xbc, dt, dtt, p["a_log"], reverse=False)
    return _merge_call(y_c, y_f, y_b, xbc, sz, gate, p["d_skip_rep"], p["ssm_norm_w"], p["w_branch"])


def kernel(x_prompt, x_sample, norm_w, w_in, b_gate, dw_w, dw_b, ln_g, ln_b, sconv_w, sconv_b,
           dt_bias, a_log, d_skip, ssm_norm_w, w_branch, w_out, final_norm_w):
    depth = norm_w.shape[0]
    assert depth == 1, "the final norm is fused into the single layer's output projection"
    w = w_in[0]
    seg = lambda lo, hi: w[:, lo:hi].astype(BF16)
    p = dict(norm_w=norm_w[0], b_gate=b_gate[0].reshape(1, -1), dw_w=dw_w[0], dw_b=dw_b[0],
             ln_g=ln_g[0], ln_b=ln_b[0], sconv_w=sconv_w[0], sconv_b=sconv_b[0], dt_bias=dt_bias[0].reshape(-1),
             a_log=a_log[0], d_skip_rep=jnp.repeat(d_skip[0], HEAD_DIM).reshape(1, SSM_DIM),
             ssm_norm_w=ssm_norm_w[0].reshape(1, SSM_DIM),
             w_cv=seg(OFF_CV, OFF_CG), w_cg=seg(OFF_CG, OFF_CS), w_cs=seg(OFF_CS, OFF_Z), w_z=seg(OFF_Z, OFF_XBC),
             w_xbc=seg(OFF_XBC, OFF_DT), w_dt=seg(OFF_DT, OFF_GATE), w_dt_t=seg(OFF_DT, OFF_GATE).T,
             w_gate=seg(OFF_GATE, IN_DIM), w_branch=w_branch[0].astype(BF16))
    w_out_b = w_out[0].astype(BF16)

    def trunk(x):
        bsz, s, d = x.shape
        outs = [_out_call(x[b], _layer(x[b], p), w_out_b, final_norm_w) for b in range(bsz)]
        return outs[0].reshape(1, s, d) if bsz == 1 else jnp.stack(outs, axis=0)

    return trunk(x_prompt), trunk(x_sample)
```

```python
import functools
import math

import jax
import jax.numpy as jnp
import numpy as np
from jax import lax
from jax.experimental import pallas as pl
from jax.experimental.pallas import tpu as pltpu

F32 = jnp.float32
BF16 = jnp.bfloat16

D_MODEL = 2048
CONV_DIM = D_MODEL
CONV_K = 31
SSM_DIM = 2 * D_MODEL
HEAD_DIM = 64
N_HEADS = SSM_DIM // HEAD_DIM
N_GROUPS = 8
HEADS_PER_GROUP = N_HEADS // N_GROUPS
D_STATE = 128
SSM_CONV_K = 5
BC_DIM = N_GROUPS * D_STATE
XBC_DIM = SSM_DIM + 2 * BC_DIM
NORM_GROUP = SSM_DIM // N_GROUPS
GROUP_W = HEADS_PER_GROUP * HEAD_DIM
EPS = 1e-5
OFF_CV = 0
OFF_CG = OFF_CV + CONV_DIM
OFF_CS = OFF_CG + CONV_DIM
OFF_Z = OFF_CS + CONV_DIM
OFF_XBC = OFF_Z + SSM_DIM
OFF_DT = OFF_XBC + XBC_DIM
OFF_GATE = OFF_DT + 2 * N_HEADS
IN_DIM = OFF_GATE + 2 * D_MODEL

LANES = 128
HALO = 16
SCAN_L = 128
LOG2E = math.log2(math.e)
EXP2_CLAMP = 127.0
VMEM_LIMIT = 56 * 1024 * 1024


def _cparams(*sem):
    return pltpu.CompilerParams(dimension_semantics=sem, vmem_limit_bytes=VMEM_LIMIT)


def _sigmoid(v):
    return 1.0 / (1.0 + jnp.exp(-v))


def _silu(v):
    return v * _sigmoid(v)


def _softplus(v):
    return jnp.maximum(v, 0.0) + jnp.log(1.0 + jnp.exp(-jnp.abs(v)))


def _split3(v):
    a1 = v.astype(BF16)
    r1 = v - a1.astype(F32)
    a2 = r1.astype(BF16)
    a3 = (r1 - a2.astype(F32)).astype(BF16)
    return a1, a2, a3


def _dot(a, b):
    return jnp.dot(a, b, preferred_element_type=F32)


def _dot_nt(a, b):
    return lax.dot_general(a, b, (((1,), (1,)), ((), ())), preferred_element_type=F32)


def _dot_tn(a, b):
    return lax.dot_general(a, b, (((0,), (0,)), ((), ())), preferred_element_type=F32)


def _rms_kernel(x_ref, w_ref, o_ref):
    x = x_ref[...]
    ms = jnp.mean(x * x, axis=-1, keepdims=True)
    o_ref[...] = (x * lax.rsqrt(ms + EPS) * w_ref[...]).astype(o_ref.dtype)


def _rms_call(x, w, tm=512):
    s, d = x.shape
    tm = min(tm, s)
    return pl.pallas_call(
        _rms_kernel,
        grid=(s // tm,),
        in_specs=[pl.BlockSpec((tm, d), lambda i: (i, 0)), pl.BlockSpec((1, d), lambda i: (0, 0))],
        out_specs=pl.BlockSpec((tm, d), lambda i: (i, 0)),
        out_shape=jax.ShapeDtypeStruct((s, d), BF16),
        compiler_params=_cparams("parallel"),
        name="rms_in",
    )(x, w.reshape(1, d))


def _proj_kernel(h_ref, *refs, n_w, n_extra, epilogue):
    w_refs = refs[:n_w]
    extra = refs[n_w:n_w + n_extra]
    out_refs = refs[n_w + n_extra:]
    h = h_ref[...]
    accs = [_dot(h, w[...]) for w in w_refs]
    outs = epilogue(*accs, *[e[...] for e in extra])
    for o_ref, o in zip(out_refs, outs):
        o_ref[...] = o.astype(o_ref.dtype)


def _proj_call(h, ws, extras, epilogue, n_out, name, tn, tm=1024):
    s, d = h.shape
    tm = min(tm, s)
    n = ws[0].shape[1]
    return pl.pallas_call(
        functools.partial(_proj_kernel, n_w=len(ws), n_extra=len(extras), epilogue=epilogue),
        grid=(s // tm, n // tn),
        in_specs=([pl.BlockSpec((tm, d), lambda i, j: (i, 0))]
                  + [pl.BlockSpec((d, tn), lambda i, j: (0, j)) for _ in ws]
                  + [pl.BlockSpec((1, tn), lambda i, j: (0, j)) for _ in extras]),
        out_specs=[pl.BlockSpec((tm, tn), lambda i, j: (i, j)) for _ in range(n_out)],
        out_shape=[jax.ShapeDtypeStruct((s, n), BF16) for _ in range(n_out)],
        compiler_params=_cparams("parallel", "arbitrary"),
        name=name,
    )(h, *ws, *extras)


def _epi_glu(cv, cg, cs):
    return cv * _sigmoid(cg), _silu(cs)


def _epi_silu(z):
    return (_silu(z),)


def _epi_gate(v, b):
    return (_sigmoid(v + b),)


def _dt_kernel(h_ref, w_ref, wt_ref, b_ref, bt_ref, dt_ref, dtt_ref):
    h = h_ref[...]
    dt_ref[...] = _softplus(_dot(h, w_ref[...]) + b_ref[...])
    dtt_ref[...] = _softplus(_dot_nt(wt_ref[...], h) + bt_ref[...])


def _dt_call(h, w_dt, w_dt_t, bias, tm=1024):
    s, d = h.shape
    tm = min(tm, s)
    n = w_dt.shape[1]
    return pl.pallas_call(
        _dt_kernel,
        grid=(s // tm,),
        in_specs=[pl.BlockSpec((tm, d), lambda i: (i, 0)),
                  pl.BlockSpec((d, n), lambda i: (0, 0)),
                  pl.BlockSpec((n, d), lambda i: (0, 0)),
                  pl.BlockSpec((1, n), lambda i: (0, 0)),
                  pl.BlockSpec((n, 1), lambda i: (0, 0))],
        out_specs=[pl.BlockSpec((tm, n), lambda i: (i, 0)), pl.BlockSpec((n, tm), lambda i: (0, i))],
        out_shape=[jax.ShapeDtypeStruct((s, n), F32), jax.ShapeDtypeStruct((n, s), F32)],
        compiler_params=_cparams("parallel"),
        name="dt_proj",
    )(h, w_dt, w_dt_t, bias.reshape(1, n), bias.reshape(n, 1))


def _halo_specs(s, ts, c, grid_rank=1):
    n = ts // HALO
    last = s // HALO - 1
    maps = [lambda i: (jnp.maximum(i * n - 1, 0), 0), lambda i: (i, 0), lambda i: (jnp.minimum((i + 1) * n, last), 0)]
    if grid_rank == 2:
        maps = [lambda i, j, f=f: f(i) for f in maps]
    return [pl.BlockSpec(shape, f) for shape, f in zip([(HALO, c), (ts, c), (HALO, c)], maps)]


def _fill_halo_buf(buf_ref, prev_ref, cur_ref, next_ref, ts, i, n_tiles):
    dt = buf_ref.dtype
    zeros = jnp.zeros((HALO, buf_ref.shape[1]), dt)
    buf_ref[pl.ds(HALO, ts), :] = cur_ref[...].astype(dt)

    @pl.when(i == 0)
    def _():
        buf_ref[pl.ds(0, HALO), :] = zeros

    @pl.when(i > 0)
    def _():
        buf_ref[pl.ds(0, HALO), :] = prev_ref[...].astype(dt)

    @pl.when(i == n_tiles - 1)
    def _():
        buf_ref[pl.ds(HALO + ts, HALO), :] = zeros

    @pl.when(i < n_tiles - 1)
    def _():
        buf_ref[pl.ds(HALO + ts, HALO), :] = next_ref[...].astype(dt)


SUBLANES = 8


def _load_rows(ref, start, rows, lanes):
    s = start % SUBLANES
    if s == 0:
        return ref[pl.ds(start, rows), lanes]
    base = start - s
    shape3 = (rows // SUBLANES, SUBLANES, LANES)
    lo = ref[pl.ds(base, rows), lanes].reshape(shape3)
    hi = ref[pl.ds(base + SUBLANES, rows), lanes].reshape(shape3)
    sub = lax.broadcasted_iota(jnp.int32, (1, SUBLANES, LANES), 1)
    mixed = jnp.where(sub >= s, lo, hi)
    return pltpu.roll(mixed, SUBLANES - s, 1).reshape(rows, LANES)


def _dwconv_block(buf_ref, w_ref, b_ref, lanes, r0, rows, k_taps):
    pad = k_taps // 2
    acc = jnp.broadcast_to(b_ref[:, lanes], (rows, LANES))
    for k in range(k_taps):
        acc = acc + w_ref[k:k + 1, lanes] * _load_rows(buf_ref, HALO - pad + r0 + k, rows, lanes)
    return acc


def _proj_sconv_kernel(hp_ref, hc_ref, hn_ref, w_ref, cw_ref, cb_ref, o_ref, hcat_ref, *, tm, tn):
    @pl.when(pl.program_id(1) == 0)
    def _():
        _fill_halo_buf(hcat_ref, hp_ref, hc_ref, hn_ref, tm, pl.program_id(0), pl.num_programs(0))

    nq = tm // SUBLANES
    a3 = _dot(hcat_ref[...], w_ref[...]).reshape((tm + 2 * HALO) // SUBLANES, SUBLANES, tn)
    sub = lax.broadcasted_iota(jnp.int32, (1, SUBLANES, tn), 1)
    out = jnp.broadcast_to(cb_ref[...].reshape(1, 1, tn), (nq, SUBLANES, tn))
    for k in range(SSM_CONV_K):
        q0, s = divmod(HALO - SSM_CONV_K // 2 + k, SUBLANES)
        win = a3[q0:q0 + nq]
        if s:
            win = pltpu.roll(jnp.where(sub >= s, win, a3[q0 + 1:q0 + 1 + nq]), SUBLANES - s, 1)
        out = out + cw_ref[k:k + 1, :].reshape(1, 1, tn) * win
    o_ref[...] = _silu(out).reshape(tm, tn).astype(o_ref.dtype)


def _proj_sconv_call(h, w, cw, cb, tm=1024, tn=1024):
    s, d = h.shape
    tm = min(tm, s)
    n = w.shape[1]
    return pl.pallas_call(
        functools.partial(_proj_sconv_kernel, tm=tm, tn=tn),
        grid=(s // tm, n // tn),
        in_specs=_halo_specs(s, tm, d, grid_rank=2) + [
            pl.BlockSpec((d, tn), lambda i, j: (0, j)),
            pl.BlockSpec((SSM_CONV_K, tn), lambda i, j: (0, j)),
            pl.BlockSpec((1, tn), lambda i, j: (0, j))],
        out_specs=pl.BlockSpec((tm, tn), lambda i, j: (i, j)),
        out_shape=jax.ShapeDtypeStruct((s, n), BF16),
        scratch_shapes=[pltpu.VMEM((tm + 2 * HALO, d), BF16)],
        compiler_params=_cparams("arbitrary", "arbitrary"),
        name="proj_xbc_conv",
    )(h, h, h, w, cw, cb.reshape(1, n))


CONV_ROWS = 64


def _conv_kernel(prev_ref, cur_ref, next_ref, sg_ref, w_ref, b_ref, g_ref, beta_ref, o_ref, buf_ref, c_ref, *, ts):
    _fill_halo_buf(buf_ref, prev_ref, cur_ref, next_ref, ts, pl.program_id(0), pl.num_programs(0))
    n_lane_blocks = CONV_DIM // LANES

    def lane_body(cb, carry):
        lanes = pl.ds(pl.multiple_of(cb * LANES, LANES), LANES)
        for rb in range(ts // CONV_ROWS):
            c_ref[pl.ds(rb * CONV_ROWS, CONV_ROWS), lanes] = _dwconv_block(
                buf_ref, w_ref, b_ref, lanes, rb * CONV_ROWS, CONV_ROWS, CONV_K)
        return carry

    lax.fori_loop(0, n_lane_blocks, lane_body, 0)

    def row_body(rb, carry):
        rows = pl.ds(pl.multiple_of(rb * CONV_ROWS, CONV_ROWS), CONV_ROWS)
        c = c_ref[rows, :]
        mu = jnp.mean(c, axis=-1, keepdims=True)
        xc = c - mu
        var = jnp.mean(xc * xc, axis=-1, keepdims=True)
        y = xc * lax.rsqrt(var + EPS) * g_ref[...] + beta_ref[...]
        o_ref[rows, :] = (_silu(y) * sg_ref[rows, :].astype(F32)).astype(o_ref.dtype)
        return carry

    lax.fori_loop(0, ts // CONV_ROWS, row_body, 0)


def _conv_call(u, sg, dw_w, dw_b, ln_g, ln_b, ts=512):
    s, c = u.shape
    ts = min(ts, s)
    row = lambda v: v.reshape(1, c)
    const = lambda shape: pl.BlockSpec(shape, lambda i: (0, 0))
    return pl.pallas_call(
        functools.partial(_conv_kernel, ts=ts),
        grid=(s // ts,),
        in_specs=_halo_specs(s, ts, c) + [pl.BlockSpec((ts, c), lambda i: (i, 0)),
                                          const((CONV_K, c)), const((1, c)), const((1, c)), const((1, c))],
        out_specs=pl.BlockSpec((ts, c), lambda i: (i, 0)),
        out_shape=jax.ShapeDtypeStruct((s, c), BF16),
        scratch_shapes=[pltpu.VMEM((ts + 2 * HALO, c), F32), pltpu.VMEM((ts, c), F32)],
        compiler_params=_cparams("arbitrary"),
        name="conv_branch",
    )(u, u, u, sg, dw_w, row(dw_b), row(ln_g), row(ln_b))


N_SCAN_IN = 10
N_SCAN_SCRATCH = 7


def _scan_kernel(*refs):
    fwd_in, bwd_in = refs[:N_SCAN_IN], refs[N_SCAN_IN:2 * N_SCAN_IN]
    o_f, o_b = refs[2 * N_SCAN_IN:2 * N_SCAN_IN + 2]
    scr = refs[2 * N_SCAN_IN + 2:]
    scr_f, scr_b = scr[:N_SCAN_SCRATCH], scr[N_SCAN_SCRATCH:]

    @pl.when(pl.program_id(0) == 0)
    def _():
        scr_f[0][...] = jnp.zeros_like(scr_f[0])
        scr_b[0][...] = jnp.zeros_like(scr_b[0])

    _scan_chunk(*fwd_in, o_f, *scr_f, reverse=False)
    _scan_chunk(*bwd_in, o_b, *scr_b, reverse=True)


def _scan_chunk(x_ref, b_ref, c_ref, dt_ref, dtt_ref, alog_ref, alogt_ref, tri_ref, trit_ref, e_ref,
                o_ref, state_ref, cb_ref, yi_ref, cs_ref, cstm_ref, prep_ref, wrep_ref, *, reverse):
    L = SCAN_L
    nh2 = 2 * N_HEADS
    off = N_HEADS if reverse else 0
    last = 0 if reverse else L - 1

    tri_f = tri_ref[...].astype(F32)
    for g in range(N_GROUPS):
        bg = b_ref[:, g * D_STATE:(g + 1) * D_STATE]
        cg = c_ref[:, g * D_STATE:(g + 1) * D_STATE]
        cb_ref[g] = _dot_nt(cg, bg) * tri_f
        yi_ref[:, g * GROUP_W:(g + 1) * GROUP_W] = _dot(cg, state_ref[g].astype(BF16))

    dt = dt_ref[...]
    dtt = dtt_ref[...]
    a = dt * (-jnp.exp(alog_ref[...]) * LOG2E)
    at = dtt * (-jnp.exp(alogt_ref[...]) * LOG2E)
    p = _dot(tri_ref[...], jnp.concatenate(_split3(a), axis=1))
    cs = p[:, :nh2] + p[:, nh2:2 * nh2] + p[:, 2 * nh2:]
    pt = _dot(jnp.concatenate(_split3(at), axis=0), trit_ref[...])
    cst = pt[:nh2] + pt[nh2:2 * nh2] + pt[2 * nh2:]
    cs_ref[...] = cs
    cstm_ref[...] = cst - jnp.log(dtt) * LOG2E

    cs_last = cs[last:last + 1, :]
    pcs = jnp.exp2(cs)
    w = jnp.exp2(cs_last - cs) * dt
    both = jnp.concatenate([pcs, w], axis=0)
    hi = both.astype(BF16)
    lo = (both - hi.astype(F32)).astype(BF16)
    rep = _dot(jnp.concatenate([hi, lo], axis=1), e_ref[...])
    prep_ref[...] = rep[:L]
    wrep_ref[...] = rep[L:].astype(BF16)

    for g in range(N_GROUPS):
        gs = slice(g * GROUP_W, (g + 1) * GROUP_W)
        xw = x_ref[:, gs] * wrep_ref[:, gs]
        upd = _dot_tn(b_ref[:, g * D_STATE:(g + 1) * D_STATE], xw)
        state_ref[g] = state_ref[g] * prep_ref[last:last + 1, gs] + upd

    lo_half = lax.broadcasted_iota(jnp.int32, (L, LANES), 1) < HEAD_DIM
    for g in range(N_GROUPS):
        for hp in range(HEADS_PER_GROUP // 2):
            h0 = g * HEADS_PER_GROUP + 2 * hp
            ms = []
            for h in (h0, h0 + 1):
                seg = cs_ref[:, off + h:off + h + 1] - cstm_ref[off + h:off + h + 1, :]
                ms.append((cb_ref[g] * jnp.exp2(jnp.minimum(seg, EXP2_CLAMP))).astype(BF16))
            cols = slice(h0 * HEAD_DIM, h0 * HEAD_DIM + LANES)
            xp = x_ref[:, cols]
            zero = jnp.zeros_like(xp)
            rhs = jnp.concatenate([jnp.where(lo_half, xp, zero), jnp.where(lo_half, zero, xp)], axis=0)
            y = _dot(jnp.concatenate(ms, axis=1), rhs) + yi_ref[:, cols] * prep_ref[:, cols]
            o_ref[:, cols] = y.astype(o_ref.dtype)


def _scan_consts(reverse):
    l = np.arange(SCAN_L)
    tri = (l[None, :] >= l[:, None]) if reverse else (l[None, :] <= l[:, None])
    off = N_HEADS if reverse else 0
    e = np.zeros((2, 2 * N_HEADS, SSM_DIM), np.float32)
    for h in range(N_HEADS):
        e[:, off + h, h * HEAD_DIM:(h + 1) * HEAD_DIM] = 1.0
    return (jnp.asarray(tri, dtype=BF16), jnp.asarray(tri.T, dtype=BF16),
            jnp.asarray(e.reshape(4 * N_HEADS, SSM_DIM), dtype=BF16))


def _scan_call(xbc, dt, dtt, a_log):
    s = xbc.shape[0]
    nc = s // SCAN_L
    L = SCAN_L
    alog = a_log.reshape(1, 2 * N_HEADS)
    alog_t = alog.reshape(2 * N_HEADS, 1)
    const = lambda shape: pl.BlockSpec(shape, lambda t: (0, 0))
    x_blocks = SSM_DIM // BC_DIM

    def direction(reverse):
        ch = (lambda t: nc - 1 - t) if reverse else (lambda t: t)
        tri, trit, e = _scan_consts(reverse)
        specs = [pl.BlockSpec((L, SSM_DIM), lambda t: (ch(t), 0)),
                 pl.BlockSpec((L, BC_DIM), lambda t: (ch(t), x_blocks)),
                 pl.BlockSpec((L, BC_DIM), lambda t: (ch(t), x_blocks + 1)),
                 pl.BlockSpec((L, 2 * N_HEADS), lambda t: (ch(t), 0)),
                 pl.BlockSpec((2 * N_HEADS, L), lambda t: (0, ch(t))),
                 const((1, 2 * N_HEADS)), const((2 * N_HEADS, 1)),
                 const((L, L)), const((L, L)), const((4 * N_HEADS, SSM_DIM))]
        args = [xbc, xbc, xbc, dt, dtt, alog, alog_t, tri, trit, e]
        assert len(specs) == N_SCAN_IN
        return specs, args, pl.BlockSpec((L, SSM_DIM), lambda t: (ch(t), 0))

    specs_f, args_f, out_f = direction(False)
    specs_b, args_b, out_b = direction(True)
    scratch = [pltpu.VMEM((N_GROUPS, D_STATE, GROUP_W), F32),
               pltpu.VMEM((N_GROUPS, L, L), F32), pltpu.VMEM((L, SSM_DIM), F32),
               pltpu.VMEM((L, 2 * N_HEADS), F32), pltpu.VMEM((2 * N_HEADS, L), F32),
               pltpu.VMEM((L, SSM_DIM), F32), pltpu.VMEM((L, SSM_DIM), BF16)]
    assert len(scratch) == N_SCAN_SCRATCH
    return pl.pallas_call(
        _scan_kernel,
        grid=(nc,),
        in_specs=specs_f + specs_b,
        out_specs=[out_f, out_b],
        out_shape=[jax.ShapeDtypeStruct((s, SSM_DIM), BF16)] * 2,
        scratch_shapes=scratch + scratch,
        compiler_params=_cparams("arbitrary"),
        name="ssd_scan",
    )(*args_f, *args_b)


MERGE_KB = 1024


def _merge_kernel(yc_ref, yf_ref, yb_ref, x_ref, sz_ref, dskip_ref, nw_ref, w_ref, g_ref, o_ref,
                  acc_ref, *, n_conv_steps, n_steps):
    k = pl.program_id(1)

    @pl.when(k == 0)
    def _():
        acc_ref[...] = _dot(yc_ref[...], w_ref[...])

    @pl.when(jnp.logical_and(k > 0, k < n_conv_steps))
    def _():
        acc_ref[...] += _dot(yc_ref[...], w_ref[...])

    @pl.when(k == n_conv_steps - 1)
    def _():
        o_ref[...] = (g_ref[...].astype(F32) * acc_ref[...]).astype(o_ref.dtype)

    def ssd_block():
        y = yf_ref[...].astype(F32) + yb_ref[...].astype(F32) + x_ref[...].astype(F32) * dskip_ref[...]
        y = y * sz_ref[...].astype(F32)
        parts = []
        for g in range(MERGE_KB // NORM_GROUP):
            yg = y[:, g * NORM_GROUP:(g + 1) * NORM_GROUP]
            ms = jnp.mean(yg * yg, axis=-1, keepdims=True)
            parts.append((yg * lax.rsqrt(ms + EPS) * nw_ref[:, g * NORM_GROUP:(g + 1) * NORM_GROUP]).astype(BF16))
        return _dot(jnp.concatenate(parts, axis=1), w_ref[...])

    @pl.when(k == n_conv_steps)
    def _():
        acc_ref[...] = ssd_block()

    @pl.when(k > n_conv_steps)
    def _():
        acc_ref[...] += ssd_block()

    @pl.when(k == n_steps - 1)
    def _():
        o_ref[...] = (o_ref[...].astype(F32) + g_ref[...].astype(F32) * acc_ref[...]).astype(o_ref.dtype)


def _merge_call(yc, yf, yb, xbc, sz, gate, d_skip_rep, norm_w, w_branch, tm=1024):
    s = yc.shape[0]
    tm = min(tm, s)
    kb = MERGE_KB
    nc_steps = CONV_DIM // kb
    n_steps = nc_steps + SSM_DIM // kb
    conv_k = lambda i, k: (i, jnp.minimum(k, nc_steps - 1))
    ssd_k = lambda i, k: (i, jnp.maximum(k - nc_steps, 0))
    ssd_row = lambda i, k: (0, jnp.maximum(k - nc_steps, 0))
    return pl.pallas_call(
        functools.partial(_merge_kernel, n_conv_steps=nc_steps, n_steps=n_steps),
        grid=(s // tm, n_steps),
        in_specs=[pl.BlockSpec((tm, kb), conv_k),
                  pl.BlockSpec((tm, kb), ssd_k), pl.BlockSpec((tm, kb), ssd_k),
                  pl.BlockSpec((tm, kb), ssd_k), pl.BlockSpec((tm, kb), ssd_k),
                  pl.BlockSpec((1, kb), ssd_row), pl.BlockSpec((1, kb), ssd_row),
                  pl.BlockSpec((kb, D_MODEL), lambda i, k: (k, 0)),
                  pl.BlockSpec((tm, D_MODEL), lambda i, k: (i, jnp.where(k < nc_steps, 0, 1)))],
        out_specs=pl.BlockSpec((tm, D_MODEL), lambda i, k: (i, 0)),
        out_shape=jax.ShapeDtypeStruct((s, D_MODEL), BF16),
        scratch_shapes=[pltpu.VMEM((tm, D_MODEL), F32)],
        compiler_params=_cparams("parallel", "arbitrary"),
        name="merge",
    )(yc, yf, yb, xbc, sz, d_skip_rep, norm_w, w_branch, gate)


def _out_kernel(x_ref, m_ref, w_ref, nw_ref, o_ref):
    r = x_ref[...] + _dot(m_ref[...], w_ref[...])
    ms = jnp.mean(r * r, axis=-1, keepdims=True)
    o_ref[...] = r * lax.rsqrt(ms + EPS) * nw_ref[...]


def _out_call(x, m, w_out, norm_w, tm=512):
    s, d = x.shape
    tm = min(tm, s)
    return pl.pallas_call(
        _out_kernel,
        grid=(s // tm,),
        in_specs=[pl.BlockSpec((tm, d), lambda i: (i, 0)),
                  pl.BlockSpec((tm, d), lambda i: (i, 0)),
                  pl.BlockSpec((d, d), lambda i: (0, 0)),
                  pl.BlockSpec((1, d), lambda i: (0, 0))],
        out_specs=pl.BlockSpec((tm, d), lambda i: (i, 0)),
        out_shape=jax.ShapeDtypeStruct((s, d), F32),
        compiler_params=_cparams("parallel"),
        name="out_proj",
    )(x, m, w_out, norm_w.reshape(1, d))


def _layer(x, p):
    h = _rms_call(x, p["norm_w"])
    u, sg = _proj_call(h, [p["w_cv"], p["w_cg"], p["w_cs"]], [], _epi_glu, 2, "proj_conv", tn=512)
    (sz,) = _proj_call(h, [p["w_z"]], [], _epi_silu, 1, "proj_z", tn=1024)
    (gate,) = _proj_call(h, [p["w_gate"]], [p["b_gate"]], _epi_gate, 1, "proj_gate", tn=1024)
    dt, dtt = _dt_call(h, p["w_dt"], p["w_dt_t"], p["dt_bias"])
    xbc = _proj_sconv_call(h, p["w_xbc"], p["sconv_w"], p["sconv_b"])

    y_c = _conv_call(u, sg, p["dw_w"], p["dw_b"], p["ln_g"], p["ln_b"])
    y_f, y_b = _scan_call(xbc, dt, dtt, p["a_log"])
    return _merge_call(y_c, y_f, y_b, xbc, sz, gate, p["d_skip_rep"], p["ssm_norm_w"], p["w_branch"])


def kernel(x_prompt, x_sample, norm_w, w_in, b_gate, dw_w, dw_b, ln_g, ln_b, sconv_w, sconv_b,
           dt_bias, a_log, d_skip, ssm_norm_w, w_branch, w_out, final_norm_w):
    depth = norm_w.shape[0]
    assert depth == 1, "the final norm is fused into the single layer's output projection"
    w = w_in[0]
    seg = lambda lo, hi: w[:, lo:hi].astype(BF16)
    p = dict(norm_w=norm_w[0], b_gate=b_gate[0].reshape(1, -1), dw_w=dw_w[0], dw_b=dw_b[0],
             ln_g=ln_g[0], ln_b=ln_b[0], sconv_w=sconv_w[0], sconv_b=sconv_b[0], dt_bias=dt_bias[0].reshape(-1),
             a_log=a_log[0], d_skip_rep=jnp.repeat(d_skip[0], HEAD_DIM).reshape(1, SSM_DIM),
             ssm_norm_w=ssm_norm_w[0].reshape(1, SSM_DIM),
             w_cv=seg(OFF_CV, OFF_CG), w_cg=seg(OFF_CG, OFF_CS), w_cs=seg(OFF_CS, OFF_Z), w_z=seg(OFF_Z, OFF_XBC),
             w_xbc=seg(OFF_XBC, OFF_DT), w_dt=seg(OFF_DT, OFF_GATE), w_dt_t=seg(OFF_DT, OFF_GATE).T,
             w_gate=seg(OFF_GATE, IN_DIM), w_branch=w_branch[0].astype(BF16))
    w_out_b = w_out[0].astype(BF16)

    def trunk(x):
        bsz, s, d = x.shape
        outs = [_out_call(x[b], _layer(x[b], p), w_out_b, final_norm_w) for b in range(bsz)]
        return outs[0].reshape(1, s, d) if bsz == 1 else jnp.stack(outs, axis=0)

    return trunk(x_prompt), trunk(x_sample)
```

```python
import functools
import math

import jax
import jax.numpy as jnp
import numpy as np
from jax import lax
from jax.experimental import pallas as pl
from jax.experimental.pallas import tpu as pltpu

F32 = jnp.float32
BF16 = jnp.bfloat16

D_MODEL = 2048
CONV_DIM = D_MODEL
CONV_K = 31
SSM_DIM = 2 * D_MODEL
HEAD_DIM = 64
N_HEADS = SSM_DIM // HEAD_DIM
N_GROUPS = 8
HEADS_PER_GROUP = N_HEADS // N_GROUPS
D_STATE = 128
SSM_CONV_K = 5
BC_DIM = N_GROUPS * D_STATE
XBC_DIM = SSM_DIM + 2 * BC_DIM
NORM_GROUP = SSM_DIM // N_GROUPS
GROUP_W = HEADS_PER_GROUP * HEAD_DIM
EPS = 1e-5
OFF_CV = 0
OFF_CG = OFF_CV + CONV_DIM
OFF_CS = OFF_CG + CONV_DIM
OFF_Z = OFF_CS + CONV_DIM
OFF_XBC = OFF_Z + SSM_DIM
OFF_DT = OFF_XBC + XBC_DIM
OFF_GATE = OFF_DT + 2 * N_HEADS
IN_DIM = OFF_GATE + 2 * D_MODEL

LANES = 128
HALO = 16
SCAN_L = 128
LOG2E = math.log2(math.e)
EXP2_CLAMP = 127.0
VMEM_LIMIT = 56 * 1024 * 1024


def _cparams(*sem):
    return pltpu.CompilerParams(dimension_semantics=sem, vmem_limit_bytes=VMEM_LIMIT)


def _sigmoid(v):
    return 0.5 + 0.5 * jnp.tanh(0.5 * v)


def _silu(v):
    half = 0.5 * v
    return half * (1.0 + jnp.tanh(half))


def _softplus(v):
    return jnp.maximum(v, 0.0) + jnp.log(1.0 + jnp.exp(-jnp.abs(v)))


def _split3(v):
    a1 = v.astype(BF16)
    r1 = v - a1.astype(F32)
    a2 = r1.astype(BF16)
    a3 = (r1 - a2.astype(F32)).astype(BF16)
    return a1, a2, a3


def _dot(a, b):
    return jnp.dot(a, b, preferred_element_type=F32)


def _dot_nt(a, b):
    return lax.dot_general(a, b, (((1,), (1,)), ((), ())), preferred_element_type=F32)


def _dot_tn(a, b):
    return lax.dot_general(a, b, (((0,), (0,)), ((), ())), preferred_element_type=F32)


def _rms_kernel(x_ref, w_ref, o_ref):
    x = x_ref[...]
    ms = jnp.mean(x * x, axis=-1, keepdims=True)
    o_ref[...] = (x * lax.rsqrt(ms + EPS) * w_ref[...]).astype(o_ref.dtype)


def _rms_call(x, w, tm=1024):
    s, d = x.shape
    tm = min(tm, s)
    return pl.pallas_call(
        _rms_kernel,
        grid=(s // tm,),
        in_specs=[pl.BlockSpec((tm, d), lambda i: (i, 0)), pl.BlockSpec((1, d), lambda i: (0, 0))],
        out_specs=pl.BlockSpec((tm, d), lambda i: (i, 0)),
        out_shape=jax.ShapeDtypeStruct((s, d), BF16),
        compiler_params=_cparams("parallel"),
        name="rms_in",
    )(x, w.reshape(1, d))


def _proj_kernel(h_ref, *refs, n_w, n_extra, epilogue):
    w_refs = refs[:n_w]
    extra = refs[n_w:n_w + n_extra]
    out_refs = refs[n_w + n_extra:]
    h = h_ref[...]
    accs = [_dot(h, w[...]) for w in w_refs]
    outs = epilogue(*accs, *[e[...] for e in extra])
    for o_ref, o in zip(out_refs, outs):
        o_ref[...] = o.astype(o_ref.dtype)


def _col_spec(d, tn, col0):
    assert col0 % tn == 0
    return pl.BlockSpec((d, tn), lambda i, j: (0, col0 // tn + j))


def _proj_call(h, w, col0s, n, extras, epilogue, n_out, name, tn, tm=1024):
    s, d = h.shape
    tm = min(tm, s)
    return pl.pallas_call(
        functools.partial(_proj_kernel, n_w=len(col0s), n_extra=len(extras), epilogue=epilogue),
        grid=(s // tm, n // tn),
        in_specs=([pl.BlockSpec((tm, d), lambda i, j: (i, 0))]
                  + [_col_spec(d, tn, c) for c in col0s]
                  + [pl.BlockSpec((1, tn), lambda i, j: (0, j)) for _ in extras]),
        out_specs=[pl.BlockSpec((tm, tn), lambda i, j: (i, j)) for _ in range(n_out)],
        out_shape=[jax.ShapeDtypeStruct((s, n), BF16) for _ in range(n_out)],
        compiler_params=_cparams("parallel", "arbitrary"),
        name=name,
    )(h, *[w] * len(col0s), *extras)


def _epi_glu(cv, cg, cs):
    return cv * _sigmoid(cg), _silu(cs)


def _epi_silu(z):
    return (_silu(z),)


def _epi_gate(v, b):
    return (_sigmoid(v + b),)


def _dt_kernel(h_ref, w_ref, wt_ref, b_ref, bt_ref, dt_ref, dtt_ref):
    h = h_ref[...]
    dt_ref[...] = _softplus(_dot(h, w_ref[...]) + b_ref[...])
    dtt_ref[...] = _softplus(_dot_nt(wt_ref[...], h) + bt_ref[...])


def _dt_call(h, w_dt, w_dt_t, bias, tm=1024):
    s, d = h.shape
    tm = min(tm, s)
    n = w_dt.shape[1]
    return pl.pallas_call(
        _dt_kernel,
        grid=(s // tm,),
        in_specs=[pl.BlockSpec((tm, d), lambda i: (i, 0)),
                  pl.BlockSpec((d, n), lambda i: (0, 0)),
                  pl.BlockSpec((n, d), lambda i: (0, 0)),
                  pl.BlockSpec((1, n), lambda i: (0, 0)),
                  pl.BlockSpec((n, 1), lambda i: (0, 0))],
        out_specs=[pl.BlockSpec((tm, n), lambda i: (i, 0)), pl.BlockSpec((n, tm), lambda i: (0, i))],
        out_shape=[jax.ShapeDtypeStruct((s, n), F32), jax.ShapeDtypeStruct((n, s), F32)],
        compiler_params=_cparams("parallel"),
        name="dt_proj",
    )(h, w_dt, w_dt_t, bias.reshape(1, n), bias.reshape(n, 1))


def _halo_specs(s, ts, c, grid_rank=1):
    n = ts // HALO
    last = s // HALO - 1
    maps = [lambda i: (jnp.maximum(i * n - 1, 0), 0), lambda i: (i, 0), lambda i: (jnp.minimum((i + 1) * n, last), 0)]
    if grid_rank == 2:
        maps = [lambda i, j, f=f: f(i) for f in maps]
    return [pl.BlockSpec(shape, f) for shape, f in zip([(HALO, c), (ts, c), (HALO, c)], maps)]


def _fill_halo_buf(buf_ref, prev_ref, cur_ref, next_ref, ts, i, n_tiles):
    dt = buf_ref.dtype
    zeros = jnp.zeros((HALO, buf_ref.shape[1]), dt)
    buf_ref[pl.ds(HALO, ts), :] = cur_ref[...].astype(dt)

    @pl.when(i == 0)
    def _():
        buf_ref[pl.ds(0, HALO), :] = zeros

    @pl.when(i > 0)
    def _():
        buf_ref[pl.ds(0, HALO), :] = prev_ref[...].astype(dt)

    @pl.when(i == n_tiles - 1)
    def _():
        buf_ref[pl.ds(HALO + ts, HALO), :] = zeros

    @pl.when(i < n_tiles - 1)
    def _():
        buf_ref[pl.ds(HALO + ts, HALO), :] = next_ref[...].astype(dt)


SUBLANES = 8


def _load_rows(ref, start, rows, lanes):
    s = start % SUBLANES
    if s == 0:
        return ref[pl.ds(start, rows), lanes]
    base = start - s
    shape3 = (rows // SUBLANES, SUBLANES, LANES)
    lo = ref[pl.ds(base, rows), lanes].reshape(shape3)
    hi = ref[pl.ds(base + SUBLANES, rows), lanes].reshape(shape3)
    sub = lax.broadcasted_iota(jnp.int32, (1, SUBLANES, LANES), 1)
    mixed = jnp.where(sub >= s, lo, hi)
    return pltpu.roll(mixed, SUBLANES - s, 1).reshape(rows, LANES)


def _dwconv_block(buf_ref, w_ref, b_ref, lanes, r0, rows, k_taps):
    pad = k_taps // 2
    acc = jnp.broadcast_to(b_ref[:, lanes], (rows, LANES))
    for k in range(k_taps):
        acc = acc + w_ref[k:k + 1, lanes] * _load_rows(buf_ref, HALO - pad + r0 + k, rows, lanes)
    return acc


def _proj_sconv_kernel(hp_ref, hc_ref, hn_ref, w_ref, cw_ref, cb_ref, o_ref, hcat_ref, *, tm, tn):
    @pl.when(pl.program_id(1) == 0)
    def _():
        _fill_halo_buf(hcat_ref, hp_ref, hc_ref, hn_ref, tm, pl.program_id(0), pl.num_programs(0))

    nq = tm // SUBLANES
    a3 = _dot(hcat_ref[...], w_ref[...]).reshape((tm + 2 * HALO) // SUBLANES, SUBLANES, tn)
    sub = lax.broadcasted_iota(jnp.int32, (1, SUBLANES, tn), 1)
    out = jnp.broadcast_to(cb_ref[...].reshape(1, 1, tn), (nq, SUBLANES, tn))
    for k in range(SSM_CONV_K):
        q0, s = divmod(HALO - SSM_CONV_K // 2 + k, SUBLANES)
        win = a3[q0:q0 + nq]
        if s:
            win = pltpu.roll(jnp.where(sub >= s, win, a3[q0 + 1:q0 + 1 + nq]), SUBLANES - s, 1)
        out = out + cw_ref[k:k + 1, :].reshape(1, 1, tn) * win
    o_ref[...] = _silu(out).reshape(tm, tn).astype(o_ref.dtype)


def _proj_sconv_call(h, w, col0, cw, cb, tm=1024, tn=1024):
    s, d = h.shape
    tm = min(tm, s)
    n = cw.shape[1]
    return pl.pallas_call(
        functools.partial(_proj_sconv_kernel, tm=tm, tn=tn),
        grid=(s // tm, n // tn),
        in_specs=_halo_specs(s, tm, d, grid_rank=2) + [
            _col_spec(d, tn, col0),
            pl.BlockSpec((SSM_CONV_K, tn), lambda i, j: (0, j)),
            pl.BlockSpec((1, tn), lambda i, j: (0, j))],
        out_specs=pl.BlockSpec((tm, tn), lambda i, j: (i, j)),
        out_shape=jax.ShapeDtypeStruct((s, n), BF16),
        scratch_shapes=[pltpu.VMEM((tm + 2 * HALO, d), BF16)],
        compiler_params=_cparams("arbitrary", "arbitrary"),
        name="proj_xbc_conv",
    )(h, h, h, w, cw, cb.reshape(1, n))


CONV_ROWS = 128


def _conv_kernel(prev_ref, cur_ref, next_ref, sg_ref, w_ref, b_ref, g_ref, beta_ref, o_ref, buf_ref, c_ref, *, ts):
    _fill_halo_buf(buf_ref, prev_ref, cur_ref, next_ref, ts, pl.program_id(0), pl.num_programs(0))
    n_lane_blocks = CONV_DIM // LANES

    def lane_body(cb, carry):
        lanes = pl.ds(pl.multiple_of(cb * LANES, LANES), LANES)
        for rb in range(ts // CONV_ROWS):
            c_ref[pl.ds(rb * CONV_ROWS, CONV_ROWS), lanes] = _dwconv_block(
                buf_ref, w_ref, b_ref, lanes, rb * CONV_ROWS, CONV_ROWS, CONV_K)
        return carry

    lax.fori_loop(0, n_lane_blocks, lane_body, 0)

    def row_body(rb, carry):
        rows = pl.ds(pl.multiple_of(rb * CONV_ROWS, CONV_ROWS), CONV_ROWS)
        c = c_ref[rows, :]
        mu = jnp.mean(c, axis=-1, keepdims=True)
        xc = c - mu
        var = jnp.mean(xc * xc, axis=-1, keepdims=True)
        y = xc * lax.rsqrt(var + EPS) * g_ref[...] + beta_ref[...]
        o_ref[rows, :] = (_silu(y) * sg_ref[rows, :].astype(F32)).astype(o_ref.dtype)
        return carry

    lax.fori_loop(0, ts // CONV_ROWS, row_body, 0)


def _conv_call(u, sg, dw_w, dw_b, ln_g, ln_b, ts=512):
    s, c = u.shape
    ts = min(ts, s)
    row = lambda v: v.reshape(1, c)
    const = lambda shape: pl.BlockSpec(shape, lambda i: (0, 0))
    return pl.pallas_call(
        functools.partial(_conv_kernel, ts=ts),
        grid=(s // ts,),
        in_specs=_halo_specs(s, ts, c) + [pl.BlockSpec((ts, c), lambda i: (i, 0)),
                                          const((CONV_K, c)), const((1, c)), const((1, c)), const((1, c))],
        out_specs=pl.BlockSpec((ts, c), lambda i: (i, 0)),
        out_shape=jax.ShapeDtypeStruct((s, c), BF16),
        scratch_shapes=[pltpu.VMEM((ts + 2 * HALO, c), F32), pltpu.VMEM((ts, c), F32)],
        compiler_params=_cparams("arbitrary"),
        name="conv_branch",
    )(u, u, u, sg, dw_w, row(dw_b), row(ln_g), row(ln_b))


N_SCAN_IN = 10
N_SCAN_SCRATCH = 7


def _scan_kernel(*refs):
    fwd_in, bwd_in = refs[:N_SCAN_IN], refs[N_SCAN_IN:2 * N_SCAN_IN]
    o_f, o_b = refs[2 * N_SCAN_IN:2 * N_SCAN_IN + 2]
    scr = refs[2 * N_SCAN_IN + 2:]
    scr_f, scr_b = scr[:N_SCAN_SCRATCH], scr[N_SCAN_SCRATCH:]

    @pl.when(pl.program_id(0) == 0)
    def _():
        scr_f[0][...] = jnp.zeros_like(scr_f[0])
        scr_b[0][...] = jnp.zeros_like(scr_b[0])

    _scan_chunk(*fwd_in, o_f, *scr_f, reverse=False)
    _scan_chunk(*bwd_in, o_b, *scr_b, reverse=True)


def _scan_chunk(x_ref, b_ref, c_ref, dt_ref, dtt_ref, alog_ref, alogt_ref, tri_ref, trit_ref, e_ref,
                o_ref, state_ref, cb_ref, yi_ref, cs_ref, cstm_ref, prep_ref, wrep_ref, *, reverse):
    L = SCAN_L
    nh2 = 2 * N_HEADS
    off = N_HEADS if reverse else 0
    last = 0 if reverse else L - 1

    tri_f = tri_ref[...].astype(F32)
    for g in range(N_GROUPS):
        bg = b_ref[:, g * D_STATE:(g + 1) * D_STATE]
        cg = c_ref[:, g * D_STATE:(g + 1) * D_STATE]
        cb_ref[g] = _dot_nt(cg, bg) * tri_f
        yi_ref[:, g * GROUP_W:(g + 1) * GROUP_W] = _dot(cg, state_ref[g].astype(BF16))

    dt = dt_ref[...]
    dtt = dtt_ref[...]
    a = dt * (-jnp.exp(alog_ref[...]) * LOG2E)
    at = dtt * (-jnp.exp(alogt_ref[...]) * LOG2E)
    p = _dot(tri_ref[...], jnp.concatenate(_split3(a), axis=1))
    cs = p[:, :nh2] + p[:, nh2:2 * nh2] + p[:, 2 * nh2:]
    pt = _dot(jnp.concatenate(_split3(at), axis=0), trit_ref[...])
    cst = pt[:nh2] + pt[nh2:2 * nh2] + pt[2 * nh2:]
    cs_ref[...] = cs
    cstm_ref[...] = cst - jnp.log(dtt) * LOG2E

    cs_last = cs[last:last + 1, :]
    pcs = jnp.exp2(cs)
    w = jnp.exp2(cs_last - cs) * dt
    both = jnp.concatenate([pcs, w], axis=0)
    hi = both.astype(BF16)
    lo = (both - hi.astype(F32)).astype(BF16)
    rep = _dot(jnp.concatenate([hi, lo], axis=1), e_ref[...])
    prep_ref[...] = rep[:L]
    wrep_ref[...] = rep[L:].astype(BF16)

    for g in range(N_GROUPS):
        gs = slice(g * GROUP_W, (g + 1) * GROUP_W)
        xw = x_ref[:, gs] * wrep_ref[:, gs]
        upd = _dot_tn(b_ref[:, g * D_STATE:(g + 1) * D_STATE], xw)
        state_ref[g] = state_ref[g] * prep_ref[last:last + 1, gs] + upd

    lo_half = lax.broadcasted_iota(jnp.int32, (L, LANES), 1) < HEAD_DIM
    for g in range(N_GROUPS):
        for hp in range(HEADS_PER_GROUP // 2):
            h0 = g * HEADS_PER_GROUP + 2 * hp
            ms = []
            for h in (h0, h0 + 1):
                seg = cs_ref[:, off + h:off + h + 1] - cstm_ref[off + h:off + h + 1, :]
                ms.append((cb_ref[g] * jnp.exp2(jnp.minimum(seg, EXP2_CLAMP))).astype(BF16))
            cols = slice(h0 * HEAD_DIM, h0 * HEAD_DIM + LANES)
            xp = x_ref[:, cols]
            zero = jnp.zeros_like(xp)
            rhs = jnp.concatenate([jnp.where(lo_half, xp, zero), jnp.where(lo_half, zero, xp)], axis=0)
            y = _dot(jnp.concatenate(ms, axis=1), rhs) + yi_ref[:, cols] * prep_ref[:, cols]
            o_ref[:, cols] = y.astype(o_ref.dtype)


def _scan_consts(reverse):
    l = np.arange(SCAN_L)
    tri = (l[None, :] >= l[:, None]) if reverse else (l[None, :] <= l[:, None])
    off = N_HEADS if reverse else 0
    e = np.zeros((2, 2 * N_HEADS, SSM_DIM), np.float32)
    for h in range(N_HEADS):
        e[:, off + h, h * HEAD_DIM:(h + 1) * HEAD_DIM] = 1.0
    return (jnp.asarray(tri, dtype=BF16), jnp.asarray(tri.T, dtype=BF16),
            jnp.asarray(e.reshape(4 * N_HEADS, SSM_DIM), dtype=BF16))


def _scan_call(xbc, dt, dtt, a_log):
    s = xbc.shape[0]
    nc = s // SCAN_L
    L = SCAN_L
    alog = a_log.reshape(1, 2 * N_HEADS)
    alog_t = alog.reshape(2 * N_HEADS, 1)
    const = lambda shape: pl.BlockSpec(shape, lambda t: (0, 0))
    x_blocks = SSM_DIM // BC_DIM

    def direction(reverse):
        ch = (lambda t: nc - 1 - t) if reverse else (lambda t: t)
        tri, trit, e = _scan_consts(reverse)
        specs = [pl.BlockSpec((L, SSM_DIM), lambda t: (ch(t), 0)),
                 pl.BlockSpec((L, BC_DIM), lambda t: (ch(t), x_blocks)),
                 pl.BlockSpec((L, BC_DIM), lambda t: (ch(t), x_blocks + 1)),
                 pl.BlockSpec((L, 2 * N_HEADS), lambda t: (ch(t), 0)),
                 pl.BlockSpec((2 * N_HEADS, L), lambda t: (0, ch(t))),
                 const((1, 2 * N_HEADS)), const((2 * N_HEADS, 1)),
                 const((L, L)), const((L, L)), const((4 * N_HEADS, SSM_DIM))]
        args = [xbc, xbc, xbc, dt, dtt, alog, alog_t, tri, trit, e]
        assert len(specs) == N_SCAN_IN
        return specs, args, pl.BlockSpec((L, SSM_DIM), lambda t: (ch(t), 0))

    specs_f, args_f, out_f = direction(False)
    specs_b, args_b, out_b = direction(True)
    scratch = [pltpu.VMEM((N_GROUPS, D_STATE, GROUP_W), F32),
               pltpu.VMEM((N_GROUPS, L, L), F32), pltpu.VMEM((L, SSM_DIM), F32),
               pltpu.VMEM((L, 2 * N_HEADS), F32), pltpu.VMEM((2 * N_HEADS, L), F32),
               pltpu.VMEM((L, SSM_DIM), F32), pltpu.VMEM((L, SSM_DIM), BF16)]
    assert len(scratch) == N_SCAN_SCRATCH
    return pl.pallas_call(
        _scan_kernel,
        grid=(nc,),
        in_specs=specs_f + specs_b,
        out_specs=[out_f, out_b],
        out_shape=[jax.ShapeDtypeStruct((s, SSM_DIM), BF16)] * 2,
        scratch_shapes=scratch + scratch,
        compiler_params=_cparams("arbitrary"),
        name="ssd_scan",
    )(*args_f, *args_b)


MERGE_KB = 1024


def _merge_kernel(yc_ref, yf_ref, yb_ref, x_ref, sz_ref, dskip_ref, nw_ref, w_ref, g_ref, o_ref,
                  acc_ref, *, n_conv_steps, n_steps):
    k = pl.program_id(1)

    @pl.when(k == 0)
    def _():
        acc_ref[...] = _dot(yc_ref[...], w_ref[...])

    @pl.when(jnp.logical_and(k > 0, k < n_conv_steps))
    def _():
        acc_ref[...] += _dot(yc_ref[...], w_ref[...])

    @pl.when(k == n_conv_steps - 1)
    def _():
        o_ref[...] = (g_ref[...].astype(F32) * acc_ref[...]).astype(o_ref.dtype)

    def ssd_block():
        y = yf_ref[...].astype(F32) + yb_ref[...].astype(F32) + x_ref[...].astype(F32) * dskip_ref[...]
        y = y * sz_ref[...].astype(F32)
        parts = []
        for g in range(MERGE_KB // NORM_GROUP):
            yg = y[:, g * NORM_GROUP:(g + 1) * NORM_GROUP]
            ms = jnp.mean(yg * yg, axis=-1, keepdims=True)
            parts.append((yg * lax.rsqrt(ms + EPS) * nw_ref[:, g * NORM_GROUP:(g + 1) * NORM_GROUP]).astype(BF16))
        return _dot(jnp.concatenate(parts, axis=1), w_ref[...])

    @pl.when(k == n_conv_steps)
    def _():
        acc_ref[...] = ssd_block()

    @pl.when(k > n_conv_steps)
    def _():
        acc_ref[...] += ssd_block()

    @pl.when(k == n_steps - 1)
    def _():
        o_ref[...] = (o_ref[...].astype(F32) + g_ref[...].astype(F32) * acc_ref[...]).astype(o_ref.dtype)


def _merge_call(yc, yf, yb, xbc, sz, gate, d_skip_rep, norm_w, w_branch, tm=1024):
    s = yc.shape[0]
    tm = min(tm, s)
    kb = MERGE_KB
    nc_steps = CONV_DIM // kb
    n_steps = nc_steps + SSM_DIM // kb
    conv_k = lambda i, k: (i, jnp.minimum(k, nc_steps - 1))
    ssd_k = lambda i, k: (i, jnp.maximum(k - nc_steps, 0))
    ssd_row = lambda i, k: (0, jnp.maximum(k - nc_steps, 0))
    return pl.pallas_call(
        functools.partial(_merge_kernel, n_conv_steps=nc_steps, n_steps=n_steps),
        grid=(s // tm, n_steps),
        in_specs=[pl.BlockSpec((tm, kb), conv_k),
                  pl.BlockSpec((tm, kb), ssd_k), pl.BlockSpec((tm, kb), ssd_k),
                  pl.BlockSpec((tm, kb), ssd_k), pl.BlockSpec((tm, kb), ssd_k),
                  pl.BlockSpec((1, kb), ssd_row), pl.BlockSpec((1, kb), ssd_row),
                  pl.BlockSpec((kb, D_MODEL), lambda i, k: (k, 0)),
                  pl.BlockSpec((tm, D_MODEL), lambda i, k: (i, jnp.where(k < nc_steps, 0, 1)))],
        out_specs=pl.BlockSpec((tm, D_MODEL), lambda i, k: (i, 0)),
        out_shape=jax.ShapeDtypeStruct((s, D_MODEL), BF16),
        scratch_shapes=[pltpu.VMEM((tm, D_MODEL), F32)],
        compiler_params=_cparams("parallel", "arbitrary"),
        name="merge",
    )(yc, yf, yb, xbc, sz, d_skip_rep, norm_w, w_branch, gate)


def _out_kernel(x_ref, m_ref, w_ref, nw_ref, o_ref):
    r = x_ref[...] + _dot(m_ref[...], w_ref[...])
    ms = jnp.mean(r * r, axis=-1, keepdims=True)
    o_ref[...] = r * lax.rsqrt(ms + EPS) * nw_ref[...]


def _out_call(x, m, w_out, norm_w, tm=512):
    s, d = x.shape
    tm = min(tm, s)
    return pl.pallas_call(
        _out_kernel,
        grid=(s // tm,),
        in_specs=[pl.BlockSpec((tm, d), lambda i: (i, 0)),
                  pl.BlockSpec((tm, d), lambda i: (i, 0)),
                  pl.BlockSpec((d, d), lambda i: (0, 0)),
                  pl.BlockSpec((1, d), lambda i: (0, 0))],
        out_specs=pl.BlockSpec((tm, d), lambda i: (i, 0)),
        out_shape=jax.ShapeDtypeStruct((s, d), F32),
        compiler_params=_cparams("parallel"),
        name="out_proj",
    )(x, m, w_out, norm_w.reshape(1, d))


def _layer(x, p):
    h = _rms_call(x, p["norm_w"])
    w = p["w_in"]
    u, sg = _proj_call(h, w, [OFF_CV, OFF_CG, OFF_CS], CONV_DIM, [], _epi_glu, 2, "proj_conv", tn=1024)
    (sz,) = _proj_call(h, w, [OFF_Z], SSM_DIM, [], _epi_silu, 1, "proj_z", tn=1024, tm=2048)
    (gate,) = _proj_call(h, p["w_gate"], [0], 2 * D_MODEL, [p["b_gate"]], _epi_gate, 1, "proj_gate", tn=1024, tm=2048)
    dt, dtt = _dt_call(h, p["w_dt"], p["w_dt_t"], p["dt_bias"])
    xbc = _proj_sconv_call(h, w, OFF_XBC, p["sconv_w"], p["sconv_b"])

    y_c = _conv_call(u, sg, p["dw_w"], p["dw_b"], p["ln_g"], p["ln_b"])
    y_f, y_b = _scan_call(xbc, dt, dtt, p["a_log"])
    return _merge_call(y_c, y_f, y_b, xbc, sz, gate, p["d_skip_rep"], p["ssm_norm_w"], p["w_branch"])


def kernel(x_prompt, x_sample, norm_w, w_in, b_gate, dw_w, dw_b, ln_g, ln_b, sconv_w, sconv_b,
           dt_bias, a_log, d_skip, ssm_norm_w, w_branch, w_out, final_norm_w):
    depth = norm_w.shape[0]
    assert depth == 1, "the final norm is fused into the single layer's output projection"
    w = w_in[0]
    seg = lambda lo, hi: w[:, lo:hi].astype(BF16)
    p = dict(norm_w=norm_w[0], b_gate=b_gate[0].reshape(1, -1), dw_w=dw_w[0], dw_b=dw_b[0],
             ln_g=ln_g[0], ln_b=ln_b[0], sconv_w=sconv_w[0], sconv_b=sconv_b[0], dt_bias=dt_bias[0].reshape(-1),
             a_log=a_log[0], d_skip_rep=jnp.repeat(d_skip[0], HEAD_DIM).reshape(1, SSM_DIM),
             ssm_norm_w=ssm_norm_w[0].reshape(1, SSM_DIM),
             w_in=w.astype(BF16), w_dt=seg(OFF_DT, OFF_GATE), w_dt_t=seg(OFF_DT, OFF_GATE).T,
             w_gate=seg(OFF_GATE, IN_DIM), w_branch=w_branch[0].astype(BF16))
    w_out_b = w_out[0].astype(BF16)

    def trunk(x):
        bsz, s, d = x.shape
        outs = [_out_call(x[b], _layer(x[b], p), w_out_b, final_norm_w) for b in range(bsz)]
        return outs[0].reshape(1, s, d) if bsz == 1 else jnp.stack(outs, axis=0)

    return trunk(x_prompt), trunk(x_sample)
```

```python
import functools
import math

import jax
import jax.numpy as jnp
import numpy as np
from jax import lax
from jax.experimental import pallas as pl
from jax.experimental.pallas import tpu as pltpu

F32 = jnp.float32
BF16 = jnp.bfloat16

D_MODEL = 2048
CONV_DIM = D_MODEL
CONV_K = 31
SSM_DIM = 2 * D_MODEL
HEAD_DIM = 64
N_HEADS = SSM_DIM // HEAD_DIM
N_GROUPS = 8
HEADS_PER_GROUP = N_HEADS // N_GROUPS
D_STATE = 128
SSM_CONV_K = 5
BC_DIM = N_GROUPS * D_STATE
XBC_DIM = SSM_DIM + 2 * BC_DIM
NORM_GROUP = SSM_DIM // N_GROUPS
GROUP_W = HEADS_PER_GROUP * HEAD_DIM
EPS = 1e-5
OFF_CV = 0
OFF_CG = OFF_CV + CONV_DIM
OFF_CS = OFF_CG + CONV_DIM
OFF_Z = OFF_CS + CONV_DIM
OFF_XBC = OFF_Z + SSM_DIM
OFF_DT = OFF_XBC + XBC_DIM
OFF_GATE = OFF_DT + 2 * N_HEADS
IN_DIM = OFF_GATE + 2 * D_MODEL

LANES = 128
HALO = 16
SCAN_L = 128
LOG2E = math.log2(math.e)
EXP2_CLAMP = 127.0
VMEM_LIMIT = 56 * 1024 * 1024


def _cparams(*sem):
    return pltpu.CompilerParams(dimension_semantics=sem, vmem_limit_bytes=VMEM_LIMIT)


def _sigmoid(v):
    return 0.5 + 0.5 * jnp.tanh(0.5 * v)


def _silu(v):
    half = 0.5 * v
    return half * (1.0 + jnp.tanh(half))


def _softplus(v):
    return jnp.maximum(v, 0.0) + jnp.log(1.0 + jnp.exp(-jnp.abs(v)))


def _split3(v):
    a1 = v.astype(BF16)
    r1 = v - a1.astype(F32)
    a2 = r1.astype(BF16)
    a3 = (r1 - a2.astype(F32)).astype(BF16)
    return a1, a2, a3


def _dot(a, b):
    return jnp.dot(a, b, preferred_element_type=F32)


def _dot_nt(a, b):
    return lax.dot_general(a, b, (((1,), (1,)), ((), ())), preferred_element_type=F32)


def _dot_tn(a, b):
    return lax.dot_general(a, b, (((0,), (0,)), ((), ())), preferred_element_type=F32)


def _rms_kernel(x_ref, w_ref, o_ref):
    x = x_ref[...]
    ms = jnp.mean(x * x, axis=-1, keepdims=True)
    o_ref[...] = (x * lax.rsqrt(ms + EPS) * w_ref[...]).astype(o_ref.dtype)


def _rms_call(x, w, tm=1024):
    s, d = x.shape
    tm = min(tm, s)
    return pl.pallas_call(
        _rms_kernel,
        grid=(s // tm,),
        in_specs=[pl.BlockSpec((tm, d), lambda i: (i, 0)), pl.BlockSpec((1, d), lambda i: (0, 0))],
        out_specs=pl.BlockSpec((tm, d), lambda i: (i, 0)),
        out_shape=jax.ShapeDtypeStruct((s, d), BF16),
        compiler_params=_cparams("parallel"),
        name="rms_in",
    )(x, w.reshape(1, d))


def _proj_kernel(h_ref, *refs, n_w, n_extra, epilogue):
    w_refs = refs[:n_w]
    extra = refs[n_w:n_w + n_extra]
    out_refs = refs[n_w + n_extra:]
    h = h_ref[...]
    accs = [_dot(h, w[...]) for w in w_refs]
    outs = epilogue(*accs, *[e[...] for e in extra])
    for o_ref, o in zip(out_refs, outs):
        o_ref[...] = o.astype(o_ref.dtype)


def _col_spec(d, tn, col0):
    assert col0 % tn == 0
    return pl.BlockSpec((d, tn), lambda i, j: (0, col0 // tn + j))


def _proj_call(h, w, col0s, n, extras, epilogue, n_out, name, tn, tm=1024):
    s, d = h.shape
    tm = min(tm, s)
    return pl.pallas_call(
        functools.partial(_proj_kernel, n_w=len(col0s), n_extra=len(extras), epilogue=epilogue),
        grid=(s // tm, n // tn),
        in_specs=([pl.BlockSpec((tm, d), lambda i, j: (i, 0))]
                  + [_col_spec(d, tn, c) for c in col0s]
                  + [pl.BlockSpec((1, tn), lambda i, j: (0, j)) for _ in extras]),
        out_specs=[pl.BlockSpec((tm, tn), lambda i, j: (i, j)) for _ in range(n_out)],
        out_shape=[jax.ShapeDtypeStruct((s, n), BF16) for _ in range(n_out)],
        compiler_params=_cparams("parallel", "arbitrary"),
        name=name,
    )(h, *[w] * len(col0s), *extras)


def _epi_glu(cv, cg, cs):
    return cv * _sigmoid(cg), _silu(cs)


def _epi_silu(z):
    return (_silu(z),)


def _epi_gate(v, b):
    return (_sigmoid(v + b),)


def _dt_kernel(h_ref, w_ref, b_ref, dt_ref, dtt_ref):
    dt = _softplus(_dot(h_ref[...], w_ref[...]) + b_ref[...])
    dt_ref[...] = dt
    dtt_ref[...] = dt.T


def _dt_call(h, w_dt, bias, tm=1024):
    s, d = h.shape
    tm = min(tm, s)
    n = w_dt.shape[1]
    return pl.pallas_call(
        _dt_kernel,
        grid=(s // tm,),
        in_specs=[pl.BlockSpec((tm, d), lambda i: (i, 0)),
                  pl.BlockSpec((d, n), lambda i: (0, 0)),
                  pl.BlockSpec((1, n), lambda i: (0, 0))],
        out_specs=[pl.BlockSpec((tm, n), lambda i: (i, 0)), pl.BlockSpec((n, tm), lambda i: (0, i))],
        out_shape=[jax.ShapeDtypeStruct((s, n), F32), jax.ShapeDtypeStruct((n, s), F32)],
        compiler_params=_cparams("parallel"),
        name="dt_proj",
    )(h, w_dt, bias.reshape(1, n))


def _halo_specs(s, ts, c, grid_rank=1):
    n = ts // HALO
    last = s // HALO - 1
    maps = [lambda i: (jnp.maximum(i * n - 1, 0), 0), lambda i: (i, 0), lambda i: (jnp.minimum((i + 1) * n, last), 0)]
    if grid_rank == 2:
        maps = [lambda i, j, f=f: f(i) for f in maps]
    return [pl.BlockSpec(shape, f) for shape, f in zip([(HALO, c), (ts, c), (HALO, c)], maps)]


def _fill_halo_buf(buf_ref, prev_ref, cur_ref, next_ref, ts, i, n_tiles):
    dt = buf_ref.dtype
    zeros = jnp.zeros((HALO, buf_ref.shape[1]), dt)
    buf_ref[pl.ds(HALO, ts), :] = cur_ref[...].astype(dt)

    @pl.when(i == 0)
    def _():
        buf_ref[pl.ds(0, HALO), :] = zeros

    @pl.when(i > 0)
    def _():
        buf_ref[pl.ds(0, HALO), :] = prev_ref[...].astype(dt)

    @pl.when(i == n_tiles - 1)
    def _():
        buf_ref[pl.ds(HALO + ts, HALO), :] = zeros

    @pl.when(i < n_tiles - 1)
    def _():
        buf_ref[pl.ds(HALO + ts, HALO), :] = next_ref[...].astype(dt)


SUBLANES = 8


def _load_rows(ref, start, rows, lanes):
    s = start % SUBLANES
    if s == 0:
        return ref[pl.ds(start, rows), lanes]
    base = start - s
    shape3 = (rows // SUBLANES, SUBLANES, LANES)
    lo = ref[pl.ds(base, rows), lanes].reshape(shape3)
    hi = ref[pl.ds(base + SUBLANES, rows), lanes].reshape(shape3)
    sub = lax.broadcasted_iota(jnp.int32, (1, SUBLANES, LANES), 1)
    mixed = jnp.where(sub >= s, lo, hi)
    return pltpu.roll(mixed, SUBLANES - s, 1).reshape(rows, LANES)


def _dwconv_block(buf_ref, w_ref, b_ref, lanes, r0, rows, k_taps):
    pad = k_taps // 2
    acc = jnp.broadcast_to(b_ref[:, lanes], (rows, LANES))
    for k in range(k_taps):
        acc = acc + w_ref[k:k + 1, lanes] * _load_rows(buf_ref, HALO - pad + r0 + k, rows, lanes)
    return acc


def _proj_sconv_kernel(hp_ref, hc_ref, hn_ref, w_ref, cw_ref, cb_ref, o_ref, hcat_ref, *, tm, tn):
    @pl.when(pl.program_id(1) == 0)
    def _():
        _fill_halo_buf(hcat_ref, hp_ref, hc_ref, hn_ref, tm, pl.program_id(0), pl.num_programs(0))

    nq = tm // SUBLANES
    a3 = _dot(hcat_ref[...], w_ref[...]).reshape((tm + 2 * HALO) // SUBLANES, SUBLANES, tn)
    sub = lax.broadcasted_iota(jnp.int32, (1, SUBLANES, tn), 1)
    out = jnp.broadcast_to(cb_ref[...].reshape(1, 1, tn), (nq, SUBLANES, tn))
    for k in range(SSM_CONV_K):
        q0, s = divmod(HALO - SSM_CONV_K // 2 + k, SUBLANES)
        win = a3[q0:q0 + nq]
        if s:
            win = pltpu.roll(jnp.where(sub >= s, win, a3[q0 + 1:q0 + 1 + nq]), SUBLANES - s, 1)
        out = out + cw_ref[k:k + 1, :].reshape(1, 1, tn) * win
    o_ref[...] = _silu(out).reshape(tm, tn).astype(o_ref.dtype)


def _proj_sconv_call(h, w, col0, cw, cb, tm=1024, tn=1024):
    s, d = h.shape
    tm = min(tm, s)
    n = cw.shape[1]
    return pl.pallas_call(
        functools.partial(_proj_sconv_kernel, tm=tm, tn=tn),
        grid=(s // tm, n // tn),
        in_specs=_halo_specs(s, tm, d, grid_rank=2) + [
            _col_spec(d, tn, col0),
            pl.BlockSpec((SSM_CONV_K, tn), lambda i, j: (0, j)),
            pl.BlockSpec((1, tn), lambda i, j: (0, j))],
        out_specs=pl.BlockSpec((tm, tn), lambda i, j: (i, j)),
        out_shape=jax.ShapeDtypeStruct((s, n), BF16),
        scratch_shapes=[pltpu.VMEM((tm + 2 * HALO, d), BF16)],
        compiler_params=_cparams("arbitrary", "arbitrary"),
        name="proj_xbc_conv",
    )(h, h, h, w, cw, cb.reshape(1, n))


CONV_ROWS = 128


def _conv_kernel(prev_ref, cur_ref, next_ref, sg_ref, w_ref, b_ref, g_ref, beta_ref, o_ref, buf_ref, c_ref, *, ts):
    _fill_halo_buf(buf_ref, prev_ref, cur_ref, next_ref, ts, pl.program_id(0), pl.num_programs(0))
    n_lane_blocks = CONV_DIM // LANES

    def lane_body(cb, carry):
        lanes = pl.ds(pl.multiple_of(cb * LANES, LANES), LANES)
        for rb in range(ts // CONV_ROWS):
            c_ref[pl.ds(rb * CONV_ROWS, CONV_ROWS), lanes] = _dwconv_block(
                buf_ref, w_ref, b_ref, lanes, rb * CONV_ROWS, CONV_ROWS, CONV_K)
        return carry

    lax.fori_loop(0, n_lane_blocks, lane_body, 0)

    def row_body(rb, carry):
        rows = pl.ds(pl.multiple_of(rb * CONV_ROWS, CONV_ROWS), CONV_ROWS)
        c = c_ref[rows, :]
        mu = jnp.mean(c, axis=-1, keepdims=True)
        xc = c - mu
        var = jnp.mean(xc * xc, axis=-1, keepdims=True)
        y = xc * lax.rsqrt(var + EPS) * g_ref[...] + beta_ref[...]
        o_ref[rows, :] = (_silu(y) * sg_ref[rows, :].astype(F32)).astype(o_ref.dtype)
        return carry

    lax.fori_loop(0, ts // CONV_ROWS, row_body, 0)


def _conv_call(u, sg, dw_w, dw_b, ln_g, ln_b, ts=512):
    s, c = u.shape
    ts = min(ts, s)
    row = lambda v: v.reshape(1, c)
    const = lambda shape: pl.BlockSpec(shape, lambda i: (0, 0))
    return pl.pallas_call(
        functools.partial(_conv_kernel, ts=ts),
        grid=(s // ts,),
        in_specs=_halo_specs(s, ts, c) + [pl.BlockSpec((ts, c), lambda i: (i, 0)),
                                          const((CONV_K, c)), const((1, c)), const((1, c)), const((1, c))],
        out_specs=pl.BlockSpec((ts, c), lambda i: (i, 0)),
        out_shape=jax.ShapeDtypeStruct((s, c), BF16),
        scratch_shapes=[pltpu.VMEM((ts + 2 * HALO, c), F32), pltpu.VMEM((ts, c), F32)],
        compiler_params=_cparams("arbitrary"),
        name="conv_branch",
    )(u, u, u, sg, dw_w, row(dw_b), row(ln_g), row(ln_b))


N_SCAN_IN = 10
N_SCAN_SCRATCH = 7


def _scan_kernel(*refs):
    fwd_in, bwd_in = refs[:N_SCAN_IN], refs[N_SCAN_IN:2 * N_SCAN_IN]
    o_f, o_b = refs[2 * N_SCAN_IN:2 * N_SCAN_IN + 2]
    scr = refs[2 * N_SCAN_IN + 2:]
    scr_f, scr_b = scr[:N_SCAN_SCRATCH], scr[N_SCAN_SCRATCH:]

    @pl.when(pl.program_id(0) == 0)
    def _():
        scr_f[0][...] = jnp.zeros_like(scr_f[0])
        scr_b[0][...] = jnp.zeros_like(scr_b[0])

    _scan_chunk(*fwd_in, o_f, *scr_f, reverse=False)
    _scan_chunk(*bwd_in, o_b, *scr_b, reverse=True)


def _scan_chunk(x_ref, b_ref, c_ref, dt_ref, dtt_ref, alog_ref, alogt_ref, tri_ref, trit_ref, e_ref,
                o_ref, state_ref, cb_ref, yi_ref, cs_ref, cstm_ref, prep_ref, wrep_ref, *, reverse):
    L = SCAN_L
    nh2 = 2 * N_HEADS
    off = N_HEADS if reverse else 0
    last = 0 if reverse else L - 1

    tri_f = tri_ref[...].astype(F32)
    for g in range(N_GROUPS):
        bg = b_ref[:, g * D_STATE:(g + 1) * D_STATE]
        cg = c_ref[:, g * D_STATE:(g + 1) * D_STATE]
        cb_ref[g] = _dot_nt(cg, bg) * tri_f
        yi_ref[:, g * GROUP_W:(g + 1) * GROUP_W] = _dot(cg, state_ref[g].astype(BF16))

    dt = dt_ref[...]
    dtt = dtt_ref[...]
    a = dt * (-jnp.exp(alog_ref[...]) * LOG2E)
    at = dtt * (-jnp.exp(alogt_ref[...]) * LOG2E)
    p = _dot(tri_ref[...], jnp.concatenate(_split3(a), axis=1))
    cs = p[:, :nh2] + p[:, nh2:2 * nh2] + p[:, 2 * nh2:]
    pt = _dot(jnp.concatenate(_split3(at), axis=0), trit_ref[...])
    cst = pt[:nh2] + pt[nh2:2 * nh2] + pt[2 * nh2:]
    cs_ref[...] = cs
    cstm_ref[...] = cst - jnp.log(dtt) * LOG2E

    cs_last = cs[last:last + 1, :]
    pcs = jnp.exp2(cs)
    w = jnp.exp2(cs_last - cs) * dt
    both = jnp.concatenate([pcs, w], axis=0)
    hi = both.astype(BF16)
    lo = (both - hi.astype(F32)).astype(BF16)
    rep = _dot(jnp.concatenate([hi, lo], axis=1), e_ref[...])
    prep_ref[...] = rep[:L]
    wrep_ref[...] = rep[L:].astype(BF16)

    for g in range(N_GROUPS):
        gs = slice(g * GROUP_W, (g + 1) * GROUP_W)
        xw = x_ref[:, gs] * wrep_ref[:, gs]
        upd = _dot_tn(b_ref[:, g * D_STATE:(g + 1) * D_STATE], xw)
        state_ref[g] = state_ref[g] * prep_ref[last:last + 1, gs] + upd

    lo_half = lax.broadcasted_iota(jnp.int32, (L, LANES), 1) < HEAD_DIM
    for g in range(N_GROUPS):
        for hp in range(HEADS_PER_GROUP // 2):
            h0 = g * HEADS_PER_GROUP + 2 * hp
            ms = []
            for h in (h0, h0 + 1):
                seg = cs_ref[:, off + h:off + h + 1] - cstm_ref[off + h:off + h + 1, :]
                ms.append((cb_ref[g] * jnp.exp2(jnp.minimum(seg, EXP2_CLAMP))).astype(BF16))
            cols = slice(h0 * HEAD_DIM, h0 * HEAD_DIM + LANES)
            xp = x_ref[:, cols]
            zero = jnp.zeros_like(xp)
            rhs = jnp.concatenate([jnp.where(lo_half, xp, zero), jnp.where(lo_half, zero, xp)], axis=0)
            y = _dot(jnp.concatenate(ms, axis=1), rhs) + yi_ref[:, cols] * prep_ref[:, cols]
            o_ref[:, cols] = y.astype(o_ref.dtype)


def _scan_consts(reverse):
    l = np.arange(SCAN_L)
    tri = (l[None, :] >= l[:, None]) if reverse else (l[None, :] <= l[:, None])
    off = N_HEADS if reverse else 0
    e = np.zeros((2, 2 * N_HEADS, SSM_DIM), np.float32)
    for h in range(N_HEADS):
        e[:, off + h, h * HEAD_DIM:(h + 1) * HEAD_DIM] = 1.0
    return (jnp.asarray(tri, dtype=BF16), jnp.asarray(tri.T, dtype=BF16),
            jnp.asarray(e.reshape(4 * N_HEADS, SSM_DIM), dtype=BF16))


def _scan_call(xbc, dt, dtt, a_log):
    s = xbc.shape[0]
    nc = s // SCAN_L
    L = SCAN_L
    alog = a_log.reshape(1, 2 * N_HEADS)
    alog_t = alog.reshape(2 * N_HEADS, 1)
    const = lambda shape: pl.BlockSpec(shape, lambda t: (0, 0))
    x_blocks = SSM_DIM // BC_DIM

    def direction(reverse):
        ch = (lambda t: nc - 1 - t) if reverse else (lambda t: t)
        tri, trit, e = _scan_consts(reverse)
        specs = [pl.BlockSpec((L, SSM_DIM), lambda t: (ch(t), 0)),
                 pl.BlockSpec((L, BC_DIM), lambda t: (ch(t), x_blocks)),
                 pl.BlockSpec((L, BC_DIM), lambda t: (ch(t), x_blocks + 1)),
                 pl.BlockSpec((L, 2 * N_HEADS), lambda t: (ch(t), 0)),
                 pl.BlockSpec((2 * N_HEADS, L), lambda t: (0, ch(t))),
                 const((1, 2 * N_HEADS)), const((2 * N_HEADS, 1)),
                 const((L, L)), const((L, L)), const((4 * N_HEADS, SSM_DIM))]
        args = [xbc, xbc, xbc, dt, dtt, alog, alog_t, tri, trit, e]
        assert len(specs) == N_SCAN_IN
        return specs, args, pl.BlockSpec((L, SSM_DIM), lambda t: (ch(t), 0))

    specs_f, args_f, out_f = direction(False)
    specs_b, args_b, out_b = direction(True)
    scratch = [pltpu.VMEM((N_GROUPS, D_STATE, GROUP_W), F32),
               pltpu.VMEM((N_GROUPS, L, L), F32), pltpu.VMEM((L, SSM_DIM), F32),
               pltpu.VMEM((L, 2 * N_HEADS), F32), pltpu.VMEM((2 * N_HEADS, L), F32),
               pltpu.VMEM((L, SSM_DIM), F32), pltpu.VMEM((L, SSM_DIM), BF16)]
    assert len(scratch) == N_SCAN_SCRATCH
    return pl.pallas_call(
        _scan_kernel,
        grid=(nc,),
        in_specs=specs_f + specs_b,
        out_specs=[out_f, out_b],
        out_shape=[jax.ShapeDtypeStruct((s, SSM_DIM), BF16)] * 2,
        scratch_shapes=scratch + scratch,
        compiler_params=_cparams("arbitrary"),
        name="ssd_scan",
    )(*args_f, *args_b)


MERGE_KB = 1024


def _merge_kernel(yc_ref, yf_ref, yb_ref, x_ref, sz_ref, dskip_ref, nw_ref, w_ref, g_ref, o_ref,
                  acc_ref, *, n_conv_steps, n_steps):
    k = pl.program_id(1)

    @pl.when(k == 0)
    def _():
        acc_ref[...] = _dot(yc_ref[...], w_ref[...])

    @pl.when(jnp.logical_and(k > 0, k < n_conv_steps))
    def _():
        acc_ref[...] += _dot(yc_ref[...], w_ref[...])

    @pl.when(k == n_conv_steps - 1)
    def _():
        o_ref[...] = (g_ref[...].astype(F32) * acc_ref[...]).astype(o_ref.dtype)

    def ssd_block():
        y = yf_ref[...].astype(F32) + yb_ref[...].astype(F32) + x_ref[...].astype(F32) * dskip_ref[...]
        y = y * sz_ref[...].astype(F32)
        parts = []
        for g in range(MERGE_KB // NORM_GROUP):
            yg = y[:, g * NORM_GROUP:(g + 1) * NORM_GROUP]
            ms = jnp.mean(yg * yg, axis=-1, keepdims=True)
            parts.append((yg * lax.rsqrt(ms + EPS) * nw_ref[:, g * NORM_GROUP:(g + 1) * NORM_GROUP]).astype(BF16))
        return _dot(jnp.concatenate(parts, axis=1), w_ref[...])

    @pl.when(k == n_conv_steps)
    def _():
        acc_ref[...] = ssd_block()

    @pl.when(k > n_conv_steps)
    def _():
        acc_ref[...] += ssd_block()

    @pl.when(k == n_steps - 1)
    def _():
        o_ref[...] = (o_ref[...].astype(F32) + g_ref[...].astype(F32) * acc_ref[...]).astype(o_ref.dtype)


def _merge_call(yc, yf, yb, xbc, sz, gate, d_skip_rep, norm_w, w_branch, tm=1024):
    s = yc.shape[0]
    tm = min(tm, s)
    kb = MERGE_KB
    nc_steps = CONV_DIM // kb
    n_steps = nc_steps + SSM_DIM // kb
    conv_k = lambda i, k: (i, jnp.minimum(k, nc_steps - 1))
    ssd_k = lambda i, k: (i, jnp.maximum(k - nc_steps, 0))
    ssd_row = lambda i, k: (0, jnp.maximum(k - nc_steps, 0))
    return pl.pallas_call(
        functools.partial(_merge_kernel, n_conv_steps=nc_steps, n_steps=n_steps),
        grid=(s // tm, n_steps),
        in_specs=[pl.BlockSpec((tm, kb), conv_k),
                  pl.BlockSpec((tm, kb), ssd_k), pl.BlockSpec((tm, kb), ssd_k),
                  pl.BlockSpec((tm, kb), ssd_k), pl.BlockSpec((tm, kb), ssd_k),
                  pl.BlockSpec((1, kb), ssd_row), pl.BlockSpec((1, kb), ssd_row),
                  pl.BlockSpec((kb, D_MODEL), lambda i, k: (k, 0)),
                  pl.BlockSpec((tm, D_MODEL), lambda i, k: (i, jnp.where(k < nc_steps, 0, 1)))],
        out_specs=pl.BlockSpec((tm, D_MODEL), lambda i, k: (i, 0)),
        out_shape=jax.ShapeDtypeStruct((s, D_MODEL), BF16),
        scratch_shapes=[pltpu.VMEM((tm, D_MODEL), F32)],
        compiler_params=_cparams("parallel", "arbitrary"),
        name="merge",
    )(yc, yf, yb, xbc, sz, d_skip_rep, norm_w, w_branch, gate)


def _out_kernel(x_ref, m_ref, w_ref, nw_ref, o_ref):
    r = x_ref[...] + _dot(m_ref[...], w_ref[...])
    ms = jnp.mean(r * r, axis=-1, keepdims=True)
    o_ref[...] = r * lax.rsqrt(ms + EPS) * nw_ref[...]


def _out_call(x, m, w_out, norm_w, tm=512):
    s, d = x.shape
    tm = min(tm, s)
    return pl.pallas_call(
        _out_kernel,
        grid=(s // tm,),
        in_specs=[pl.BlockSpec((tm, d), lambda i: (i, 0)),
                  pl.BlockSpec((tm, d), lambda i: (i, 0)),
                  pl.BlockSpec((d, d), lambda i: (0, 0)),
                  pl.BlockSpec((1, d), lambda i: (0, 0))],
        out_specs=pl.BlockSpec((tm, d), lambda i: (i, 0)),
        out_shape=jax.ShapeDtypeStruct((s, d), F32),
        compiler_params=_cparams("parallel"),
        name="out_proj",
    )(x, m, w_out, norm_w.reshape(1, d))


def _layer(x, p):
    h = _rms_call(x, p["norm_w"])
    w = p["w_in"]
    u, sg = _proj_call(h, w, [OFF_CV, OFF_CG, OFF_CS], CONV_DIM, [], _epi_glu, 2, "proj_conv", tn=1024)
    (sz,) = _proj_call(h, w, [OFF_Z], SSM_DIM, [], _epi_silu, 1, "proj_z", tn=1024, tm=2048)
    (gate,) = _proj_call(h, p["w_gate"], [0], 2 * D_MODEL, [p["b_gate"]], _epi_gate, 1, "proj_gate", tn=1024, tm=2048)
    dt, dtt = _dt_call(h, p["w_dt"], p["dt_bias"])
    xbc = _proj_sconv_call(h, w, OFF_XBC, p["sconv_w"], p["sconv_b"])

    y_c = _conv_call(u, sg, p["dw_w"], p["dw_b"], p["ln_g"], p["ln_b"])
    y_f, y_b = _scan_call(xbc, dt, dtt, p["a_log"])
    return _merge_call(y_c, y_f, y_b, xbc, sz, gate, p["d_skip_rep"], p["ssm_norm_w"], p["w_branch"])


def kernel(x_prompt, x_sample, norm_w, w_in, b_gate, dw_w, dw_b, ln_g, ln_b, sconv_w, sconv_b,
           dt_bias, a_log, d_skip, ssm_norm_w, w_branch, w_out, final_norm_w):
    depth = norm_w.shape[0]
    assert depth == 1, "the final norm is fused into the single layer's output projection"
    w = w_in[0]
    seg = lambda lo, hi: w[:, lo:hi].astype(BF16)
    p = dict(norm_w=norm_w[0], b_gate=b_gate[0].reshape(1, -1), dw_w=dw_w[0], dw_b=dw_b[0],
             ln_g=ln_g[0], ln_b=ln_b[0], sconv_w=sconv_w[0], sconv_b=sconv_b[0], dt_bias=dt_bias[0].reshape(-1),
             a_log=a_log[0], d_skip_rep=jnp.repeat(d_skip[0], HEAD_DIM).reshape(1, SSM_DIM),
             ssm_norm_w=ssm_norm_w[0].reshape(1, SSM_DIM),
             w_in=w.astype(BF16), w_dt=seg(OFF_DT, OFF_GATE),
             w_gate=seg(OFF_GATE, IN_DIM), w_branch=w_branch[0].astype(BF16))
    w_out_b = w_out[0].astype(BF16)

    def trunk(x):
        bsz, s, d = x.shape
        outs = [_out_call(x[b], _layer(x[b], p), w_out_b, final_norm_w) for b in range(bsz)]
        return outs[0].reshape(1, s, d) if bsz == 1 else jnp.stack(outs, axis=0)

    return trunk(x_prompt), trunk(x_sample)
```
